```python
import jax, jax.numpy as jnp
from jax import lax
import numpy as np

D_MODEL = 1024
BATCH = 8
SEQ = 2048
DEPTH = 2

HEAD_DIM = 64
SWA_Q_HEADS = 8
SWA_KV_HEADS = 2
SWA_WINDOW = 128
MOBA_HEADS = 8
MOBA_BLOCK = 256
MOBA_TOPK = 3
MOBA_Q_CHUNK = 32
D_FF = 4 * D_MODEL
ROPE_THETA = 10000.0
NORM_EPS = 1e-6
NEG_INF = -1e30

SWA_Q_W = SWA_Q_HEADS * HEAD_DIM
SWA_KV_W = SWA_KV_HEADS * HEAD_DIM
MOBA_W = MOBA_HEADS * HEAD_DIM
IN_SPLITS = [SWA_Q_W, SWA_KV_W, SWA_KV_W, MOBA_W, MOBA_W, MOBA_W, D_MODEL, D_MODEL]
IN_W = int(sum(IN_SPLITS))
IN_OFFSETS = [int(o) for o in np.cumsum(IN_SPLITS)[:-1]]

kernel_name = "hybrid_swa_sink_moba_sqrelu_adaln"


def rms_norm(x, gain):
    xf = x.astype(jnp.float32)
    y = xf * lax.rsqrt(jnp.mean(xf * xf, axis=-1, keepdims=True) + NORM_EPS)
    return (y * gain.astype(jnp.float32)).astype(x.dtype)


def rope(x, positions):
    half = HEAD_DIM // 2
    inv_freq = ROPE_THETA ** (-jnp.arange(half, dtype=jnp.float32) / half)
    ang = positions.astype(jnp.float32)[..., None] * inv_freq
    cos = jnp.cos(ang)[:, :, None, :]
    sin = jnp.sin(ang)[:, :, None, :]
    xf = x.astype(jnp.float32)
    x1, x2 = xf[..., :half], xf[..., half:]
    out = jnp.concatenate([x1 * cos - x2 * sin, x2 * cos + x1 * sin], axis=-1)
    return out.astype(x.dtype)


def swa_attention(q, k, v, sinks):
    B, S = q.shape[:2]
    W = SWA_WINDOW
    nb = S // W
    G = SWA_Q_HEADS // SWA_KV_HEADS
    qb = q.reshape(B, nb, W, SWA_KV_HEADS, G, HEAD_DIM)

    def band(t):
        tb = t.reshape(B, nb, W, SWA_KV_HEADS, HEAD_DIM)
        prev = jnp.pad(tb[:, :-1], ((0, 0), (1, 0), (0, 0), (0, 0), (0, 0)))
        return jnp.concatenate([prev, tb], axis=2)

    kb, vb = band(k), band(v)
    s = jnp.einsum('bnqhgd,bnkhd->bnhgqk', qb, kb,
                   preferred_element_type=jnp.float32) * (HEAD_DIM ** -0.5)
    qi = jnp.arange(W)[:, None]
    kj = jnp.arange(2 * W)[None, :]
    diff = qi + W - kj
    blk = jnp.arange(nb)[:, None, None]
    valid = (diff >= 0) & (diff < W) & (blk * W - W + kj >= 0)
    s = jnp.where(valid[None, :, None, None], s, NEG_INF)
    sink = sinks.astype(jnp.float32).reshape(SWA_KV_HEADS, G)[None, None, :, :, None, None]
    sink = jnp.broadcast_to(sink, s.shape[:-1] + (1,))
    p = jax.nn.softmax(jnp.concatenate([s, sink], axis=-1), axis=-1)[..., :-1]
    o = jnp.einsum('bnhgqk,bnkhd->bnqhgd', p.astype(v.dtype), vb)
    return o.reshape(B, S, SWA_Q_W)


def _take_blocks(blocks, idx):
    return blocks[idx]


_gather_bh = jax.vmap(jax.vmap(_take_blocks))


def moba_attention(q, k, v):
    B, S = q.shape[:2]
    L = MOBA_BLOCK
    nblk = -(-S // L)
    s_pad = nblk * L
    pad = ((0, 0), (0, s_pad - S), (0, 0), (0, 0))
    kb = jnp.pad(k, pad).reshape(B, nblk, L, MOBA_HEADS, HEAD_DIM).transpose(0, 3, 1, 2, 4)
    vb = jnp.pad(v, pad).reshape(B, nblk, L, MOBA_HEADS, HEAD_DIM).transpose(0, 3, 1, 2, 4)
    k_mean = jnp.mean(kb.astype(jnp.float32), axis=3)
    qh = q.transpose(0, 2, 1, 3)
    gate = jnp.einsum('bhsd,bhnd->bhsn', qh.astype(jnp.float32), k_mean)
    qblk = jnp.arange(S) // L
    past = jnp.arange(nblk)[None, :] < qblk[:, None]
    gate = jnp.where(past[None, None], gate, NEG_INF)
    topk = min(MOBA_TOPK, nblk)
    _, sel = lax.top_k(gate, topk)
    sel_valid = sel < qblk[None, None, :, None]
    scale = HEAD_DIM ** -0.5
    QC = MOBA_Q_CHUNK
    n_chunks = S // QC

    def chunk(ci):
        start = ci * QC
        qc = lax.dynamic_slice_in_dim(qh, start, QC, axis=2)
        selc = lax.dynamic_slice_in_dim(sel, start, QC, axis=2)
        validc = lax.dynamic_slice_in_dim(sel_valid, start, QC, axis=2)
        kg = _gather_bh(kb, selc)
        vg = _gather_bh(vb, selc)
        s_sel = jnp.einsum('bhqd,bhqtld->bhqtl', qc, kg,
                           preferred_element_type=jnp.float32) * scale
        s_sel = jnp.where(validc[..., None], s_sel, NEG_INF).reshape(B, MOBA_HEADS, QC, topk * L)
        own = start // L
        k_own = lax.dynamic_index_in_dim(kb, own, axis=2, keepdims=False)
        v_own = lax.dynamic_index_in_dim(vb, own, axis=2, keepdims=False)
        s_own = jnp.einsum('bhqd,bhld->bhql', qc, k_own,
                           preferred_element_type=jnp.float32) * scale
        qpos = start + jnp.arange(QC)
        kpos = own * L + jnp.arange(L)
        s_own = jnp.where(kpos[None, :] <= qpos[:, None], s_own, NEG_INF)
        p = jax.nn.softmax(jnp.concatenate([s_sel, s_own], axis=-1), axis=-1).astype(v.dtype)
        p_sel = p[..., :topk * L].reshape(B, MOBA_HEADS, QC, topk, L)
        p_own = p[..., topk * L:]
        return (jnp.einsum('bhqtl,bhqtld->bhqd', p_sel, vg)
                + jnp.einsum('bhql,bhld->bhqd', p_own, v_own))

    out = lax.map(chunk, jnp.arange(n_chunks))
    return out.transpose(1, 0, 3, 2, 4).reshape(B, S, MOBA_W)


def adaln(h, shift, scale):
    return h * (1 + scale[:, None, :]) + shift[:, None, :]


def hybrid_layer(x, c, positions, g_mix, g_mlp, w_ada, b_ada, w_in, qn_swa, kn_swa,
                 qn_moba, kn_moba, sinks, w_o_swa, w_o_moba, w_out, w_up, w_down):
    B, S, _ = x.shape
    mod = jax.nn.silu(c) @ w_ada + b_ada
    sh1, sc1, gt1, sh2, sc2, gt2 = jnp.split(mod, 6, axis=-1)

    h = adaln(rms_norm(x, g_mix), sh1, sc1)
    proj = h @ w_in
    q_a, k_a, v_a, q_b, k_b, v_b, gate_a, gate_b = jnp.split(proj, IN_OFFSETS, axis=-1)
    q_a = rope(rms_norm(q_a.reshape(B, S, SWA_Q_HEADS, HEAD_DIM), qn_swa), positions)
    k_a = rope(rms_norm(k_a.reshape(B, S, SWA_KV_HEADS, HEAD_DIM), kn_swa), positions)
    v_a = v_a.reshape(B, S, SWA_KV_HEADS, HEAD_DIM)
    q_b = rope(rms_norm(q_b.reshape(B, S, MOBA_HEADS, HEAD_DIM), qn_moba), positions)
    k_b = rope(rms_norm(k_b.reshape(B, S, MOBA_HEADS, HEAD_DIM), kn_moba), positions)
    v_b = v_b.reshape(B, S, MOBA_HEADS, HEAD_DIM)
    y_a = swa_attention(q_a, k_a, v_a, sinks) @ w_o_swa
    y_b = moba_attention(q_b, k_b, v_b) @ w_o_moba
    mixed = jax.nn.sigmoid(gate_a) * y_a + jax.nn.sigmoid(gate_b) * y_b
    x = x + gt1[:, None, :] * (mixed @ w_out)

    h = adaln(rms_norm(x, g_mlp), sh2, sc2)
    u = jnp.square(jax.nn.relu(h @ w_up))
    x = x + gt2[:, None, :] * (u @ w_down)
    return x


def setup_inputs(seed: int = 0) -> dict:
    key = jax.random.key(seed)
    ks = jax.random.split(key, 20)
    f32 = jnp.float32

    def nrm(k, shape, scale):
        return jax.random.normal(k, shape, f32) * scale

    x = jax.random.normal(ks[0], (BATCH, SEQ, D_MODEL), f32)
    c = jax.random.normal(ks[1], (BATCH, D_MODEL), f32)
    offs = jax.random.randint(ks[2], (BATCH, 1), 0, 1024, dtype=jnp.int32)
    positions = (jnp.arange(SEQ, dtype=jnp.int32)[None, :] + offs).astype(jnp.int32)
    return {
        "x": x,
        "c": c,
        "positions": positions,
        "rms_mix": 1.0 + nrm(ks[3], (DEPTH, D_MODEL), 0.05),
        "rms_mlp": 1.0 + nrm(ks[4], (DEPTH, D_MODEL), 0.05),
        "w_ada": nrm(ks[5], (DEPTH, D_MODEL, 6 * D_MODEL), 0.5 * D_MODEL ** -0.5),
        "b_ada": nrm(ks[6], (DEPTH, 6 * D_MODEL), 0.02),
        "w_in": nrm(ks[7], (DEPTH, D_MODEL, IN_W), D_MODEL ** -0.5),
        "q_norm_swa": 1.0 + nrm(ks[8], (DEPTH, HEAD_DIM), 0.05),
        "k_norm_swa": 1.0 + nrm(ks[9], (DEPTH, HEAD_DIM), 0.05),
        "q_norm_moba": 1.0 + nrm(ks[10], (DEPTH, HEAD_DIM), 0.05),
        "k_norm_moba": 1.0 + nrm(ks[11], (DEPTH, HEAD_DIM), 0.05),
        "swa_sinks": nrm(ks[12], (DEPTH, SWA_Q_HEADS), 0.5),
        "w_o_swa": nrm(ks[13], (DEPTH, SWA_Q_W, D_MODEL), SWA_Q_W ** -0.5),
        "w_o_moba": nrm(ks[14], (DEPTH, MOBA_W, D_MODEL), MOBA_W ** -0.5),
        "w_out": nrm(ks[15], (DEPTH, D_MODEL, D_MODEL), D_MODEL ** -0.5),
        "w_up": nrm(ks[16], (DEPTH, D_MODEL, D_FF), D_MODEL ** -0.5),
        "w_down": nrm(ks[17], (DEPTH, D_FF, D_MODEL), D_FF ** -0.5),
    }


def reference(x, c, positions, rms_mix, rms_mlp, w_ada, b_ada, w_in, q_norm_swa, k_norm_swa,
              q_norm_moba, k_norm_moba, swa_sinks, w_o_swa, w_o_moba, w_out, w_up, w_down):
    for l in range(DEPTH):
        x = hybrid_layer(x, c, positions, rms_mix[l], rms_mlp[l], w_ada[l], b_ada[l], w_in[l],
                         q_norm_swa[l], k_norm_swa[l], q_norm_moba[l], k_norm_moba[l],
                         swa_sinks[l], w_o_swa[l], w_o_moba[l], w_out[l], w_up[l], w_down[l])
    return x
```

```python
import functools

import jax
import jax.numpy as jnp
import numpy as np
from jax import lax
from jax.experimental import pallas as pl
from jax.experimental.pallas import tpu as pltpu

D_MODEL = 1024
HEAD_DIM = 64
SWA_Q_HEADS = 8
SWA_KV_HEADS = 2
SWA_WINDOW = 128
MOBA_HEADS = 8
MOBA_BLOCK = 256
MOBA_TOPK = 3
D_FF = 4 * D_MODEL
ROPE_THETA = 10000.0
NORM_EPS = 1e-6
NEG_INF = -1e30

SWA_Q_W = SWA_Q_HEADS * HEAD_DIM
SWA_KV_W = SWA_KV_HEADS * HEAD_DIM
MOBA_W = MOBA_HEADS * HEAD_DIM
IN_SPLITS = (SWA_Q_W, SWA_KV_W, SWA_KV_W, MOBA_W, MOBA_W, MOBA_W, D_MODEL, D_MODEL)
IN_W = sum(IN_SPLITS)
IN_OFFSETS = tuple(int(o) for o in np.cumsum((0,) + IN_SPLITS))

LANES = 128
HEADS_PER_TILE = LANES // HEAD_DIM
QK_SCALE = HEAD_DIM ** -0.5
VMEM_LIMIT = 56 * 1024 * 1024

BF16 = jnp.bfloat16
F32 = jnp.float32


def _split_bf16(a):
    hi = a.astype(BF16)
    lo = (a - hi.astype(F32)).astype(BF16)
    return hi, lo


def _ada_kernel(c_ref, w_ref, b_ref, o_ref):
    c = c_ref[...]
    a = c * jax.nn.sigmoid(c)
    a_hi, a_lo = _split_bf16(a)
    w_hi, w_lo = _split_bf16(w_ref[0])
    acc = jnp.dot(a_hi, w_hi, preferred_element_type=F32)
    acc += jnp.dot(a_hi, w_lo, preferred_element_type=F32)
    acc += jnp.dot(a_lo, w_hi, preferred_element_type=F32)
    o_ref[0] = acc + b_ref[0]


def _ada_call(c, w_ada, b_ada):
    depth, d, n = w_ada.shape
    batch = c.shape[0]
    tn = 1024
    return pl.pallas_call(
        _ada_kernel,
        grid=(depth, n // tn),
        in_specs=[
            pl.BlockSpec((batch, d), lambda l, j: (0, 0)),
            pl.BlockSpec((1, d, tn), lambda l, j: (l, 0, j)),
            pl.BlockSpec((1, 1, tn), lambda l, j: (l, 0, j)),
        ],
        out_specs=pl.BlockSpec((1, batch, tn), lambda l, j: (l, 0, j)),
        out_shape=jax.ShapeDtypeStruct((depth, batch, n), F32),
        compiler_params=pltpu.CompilerParams(vmem_limit_bytes=VMEM_LIMIT),
        name="ada_mod",
    )(c, w_ada, b_ada.reshape(depth, 1, n))


def _rope_table_kernel(pos_ref, inv_ref, cos_ref, sin_ref):
    ang = pos_ref[0].astype(F32) * inv_ref[...]
    lane = lax.broadcasted_iota(jnp.int32, ang.shape, 1)
    first_half = (lane % HEAD_DIM) < (HEAD_DIM // 2)
    cos_ref[0] = jnp.cos(ang)
    s = jnp.sin(ang)
    sin_ref[0] = jnp.where(first_half, -s, s)


def _rope_tables(positions):
    batch, seq = positions.shape
    half = HEAD_DIM // 2
    inv_freq = ROPE_THETA ** (-jnp.arange(half, dtype=F32) / half)
    inv_tile = jnp.tile(inv_freq, LANES // half).reshape(1, LANES)
    out = jax.ShapeDtypeStruct((batch, seq, LANES), F32)
    return pl.pallas_call(
        _rope_table_kernel,
        grid=(batch,),
        in_specs=[
            pl.BlockSpec((1, seq, 1), lambda b: (b, 0, 0)),
            pl.BlockSpec((1, LANES), lambda b: (0, 0)),
        ],
        out_specs=[pl.BlockSpec((1, seq, LANES), lambda b: (b, 0, 0))] * 2,
        out_shape=[out, out],
        name="rope_tables",
    )(positions.reshape(batch, seq, 1), inv_tile)


def _rms_adaln(x, gain, shift, scale):
    ms = jnp.mean(x * x, axis=-1, keepdims=True)
    return (x * lax.rsqrt(ms + NORM_EPS) * gain) * (1.0 + scale) + shift


def _head_norm_rope(y, gain, cos, sin, lane):
    low = lane < HEAD_DIM
    y2 = y * y
    s_low = jnp.sum(jnp.where(low, y2, 0.0), axis=-1, keepdims=True)
    s_high = jnp.sum(jnp.where(low, 0.0, y2), axis=-1, keepdims=True)
    ms = jnp.where(low, s_low, s_high) * (1.0 / HEAD_DIM)
    yn = y * lax.rsqrt(ms + NORM_EPS) * gain
    first_half = (lane % HEAD_DIM) < (HEAD_DIM // 2)
    partner = jnp.where(first_half,
                        pltpu.roll(yn, LANES - HEAD_DIM // 2, 1),
                        pltpu.roll(yn, HEAD_DIM // 2, 1))
    return yn * cos + partner * sin


def _inproj_kernel(x_ref, mod_ref, g_ref, w_ref, cos_ref, sin_ref, hn_ref,
                   qa_ref, ka_ref, va_ref, qb_ref, kb_ref, vb_ref, ga_ref, gb_ref, km_ref):
    tm = x_ref.shape[0]
    h = _rms_adaln(x_ref[...], g_ref[...], mod_ref[0, 0:1, :], mod_ref[0, 1:2, :])
    hb = h.astype(BF16)
    cos = cos_ref[...]
    sin = sin_ref[...]
    lane = lax.broadcasted_iota(jnp.int32, (1, LANES), 1)
    low = lane < HEAD_DIM

    def proj(i):
        return jnp.dot(hb, w_ref[:, IN_OFFSETS[i]:IN_OFFSETS[i + 1]], preferred_element_type=F32)

    def tiles(y):
        return [y[:, t * LANES:(t + 1) * LANES] for t in range(y.shape[1] // LANES)]

    def both_halves(t):
        r = pltpu.roll(t, HEAD_DIM, 1)
        return jnp.where(low, t, r), jnp.where(low, r, t)

    for t, y in enumerate(tiles(proj(0))):
        o = _head_norm_rope(y, hn_ref[0:1, :], cos, sin, lane) * QK_SCALE
        qa_ref[:, t * LANES:(t + 1) * LANES] = o.astype(BF16)
    k0, k1 = both_halves(_head_norm_rope(proj(1), hn_ref[1:2, :], cos, sin, lane))
    ka_ref[:, 0:LANES] = k0.astype(BF16)
    ka_ref[:, LANES:2 * LANES] = k1.astype(BF16)
    v0, v1 = both_halves(proj(2))
    va_ref[:, 0:LANES] = v0.astype(BF16)
    va_ref[:, LANES:2 * LANES] = v1.astype(BF16)
    for t, y in enumerate(tiles(proj(3))):
        o = _head_norm_rope(y, hn_ref[2:3, :], cos, sin, lane) * QK_SCALE
        qb_ref[:, t * LANES:(t + 1) * LANES] = o.astype(BF16)
    for t, y in enumerate(tiles(proj(4))):
        o = _head_norm_rope(y, hn_ref[3:4, :], cos, sin, lane)
        kb_ref[:, t * LANES:(t + 1) * LANES] = o.astype(BF16)
        for j in range(tm // MOBA_BLOCK):
            blk = o[j * MOBA_BLOCK:(j + 1) * MOBA_BLOCK]
            km_ref[0, j:j + 1, t * LANES:(t + 1) * LANES] = jnp.mean(blk, axis=0, keepdims=True)
    vb_ref[...] = proj(5).astype(BF16)
    ga_ref[...] = jax.nn.sigmoid(proj(6)).astype(BF16)
    gb_ref[...] = jax.nn.sigmoid(proj(7)).astype(BF16)


def _inproj_call(x2d, mod, g_mix, w_in, cos, sin, head_gains, seq, tm):
    tokens, d = x2d.shape
    steps_per_seq = seq // tm
    nkm = tm // MOBA_BLOCK
    row = lambda i: (i, 0)
    const = lambda i: (0, 0)

    def out(width, dtype=BF16):
        return jax.ShapeDtypeStruct((tokens, width), dtype)

    return pl.pallas_call(
        _inproj_kernel,
        grid=(tokens // tm,),
        in_specs=[
            pl.BlockSpec((tm, d), row),
            pl.BlockSpec((1, 6, d), lambda i: (i // steps_per_seq, 0, 0)),
            pl.BlockSpec((1, d), const),
            pl.BlockSpec((d, IN_W), const),
            pl.BlockSpec((tm, LANES), row),
            pl.BlockSpec((tm, LANES), row),
            pl.BlockSpec((4, LANES), const),
        ],
        out_specs=[
            pl.BlockSpec((tm, SWA_Q_W), row),
            pl.BlockSpec((tm, 2 * LANES), row),
            pl.BlockSpec((tm, 2 * LANES), row),
            pl.BlockSpec((tm, MOBA_W), row),
            pl.BlockSpec((tm, MOBA_W), row),
            pl.BlockSpec((tm, MOBA_W), row),
            pl.BlockSpec((tm, d), row),
            pl.BlockSpec((tm, d), row),
            pl.BlockSpec((1, nkm, MOBA_W), lambda i: (i, 0, 0)),
        ],
        out_shape=[
            out(SWA_Q_W), out(2 * LANES), out(2 * LANES), out(MOBA_W), out(MOBA_W), out(MOBA_W),
            out(d), out(d),
            jax.ShapeDtypeStruct((tokens // tm, nkm, MOBA_W), F32),
        ],
        compiler_params=pltpu.CompilerParams(vmem_limit_bytes=VMEM_LIMIT),
        name="in_proj",
    )(x2d, mod, g_mix, w_in, cos, sin, head_gains)


def _swa_kernel(sink_ref, q_ref, k_ref, v_ref, o_ref):
    tq = q_ref.shape[1]
    nk = tq + SWA_WINDOW
    qi = pl.program_id(1)
    start = pl.multiple_of(jnp.maximum(qi * tq - SWA_WINDOW, 0), SWA_WINDOW)
    qpos = qi * tq + lax.broadcasted_iota(jnp.int32, (tq, nk), 0)
    kpos = start + lax.broadcasted_iota(jnp.int32, (tq, nk), 1)
    diff = qpos - kpos
    valid = (diff >= 0) & (diff < SWA_WINDOW)
    lane = lax.broadcasted_iota(jnp.int32, (1, LANES), 1)
    low = lane < HEAD_DIM
    group = SWA_Q_HEADS // SWA_KV_HEADS
    for t in range(SWA_Q_HEADS // HEADS_PER_TILE):
        q2 = q_ref[0, :, t * LANES:(t + 1) * LANES]
        kvh = (t * HEADS_PER_TILE) // group
        k2 = k_ref[0, pl.ds(start, nk), kvh * LANES:(kvh + 1) * LANES]
        v2 = v_ref[0, pl.ds(start, nk), kvh * LANES:(kvh + 1) * LANES]
        outs = []
        for j in range(HEADS_PER_TILE):
            sink = sink_ref[t * HEADS_PER_TILE + j]
            qh = jnp.where(low if j == 0 else ~low, q2, jnp.zeros_like(q2))
            s = lax.dot_general(qh, k2, (((1,), (1,)), ((), ())), preferred_element_type=F32)
            s = jnp.where(valid, s, NEG_INF)
            m = jnp.maximum(jnp.max(s, axis=-1, keepdims=True), sink)
            p = jnp.exp(s - m)
            denom = jnp.sum(p, axis=-1, keepdims=True) + jnp.exp(sink - m)
            o = jnp.dot(p.astype(BF16), v2, preferred_element_type=F32)
            outs.append(o / denom)
        o_ref[0, :, t * LANES:(t + 1) * LANES] = jnp.where(low, outs[0], outs[1]).astype(BF16)


def _swa_call(sinks, qa, ka, va, tq):
    batch, seq, _ = qa.shape
    return pl.pallas_call(
        _swa_kernel,
        grid=(batch, seq // tq),
        in_specs=[
            pl.BlockSpec(memory_space=pltpu.SMEM),
            pl.BlockSpec((1, tq, SWA_Q_W), lambda b, i: (b, i, 0)),
            pl.BlockSpec((1, seq, 2 * LANES), lambda b, i: (b, 0, 0)),
            pl.BlockSpec((1, seq, 2 * LANES), lambda b, i: (b, 0, 0)),
        ],
        out_specs=pl.BlockSpec((1, tq, SWA_Q_W), lambda b, i: (b, i, 0)),
        out_shape=jax.ShapeDtypeStruct((batch, seq, SWA_Q_W), BF16),
        compiler_params=pltpu.CompilerParams(vmem_limit_bytes=VMEM_LIMIT),
        name="swa_attn",
    )(sinks, qa, ka, va)


def _moba_select(gate, qi):
    blk = lax.broadcasted_iota(jnp.int32, gate.shape, 1).astype(F32)
    past = blk < qi.astype(F32)
    g = jnp.where(past, gate, NEG_INF)
    sel = jnp.zeros(gate.shape, jnp.bool_)
    for _ in range(MOBA_TOPK):
        top = jnp.max(g, axis=-1, keepdims=True)
        idx = jnp.min(jnp.where(g == top, blk, float(LANES)), axis=-1, keepdims=True)
        hit = blk == idx
        sel = sel | hit
        g = jnp.where(hit, -jnp.inf, g)
    return jnp.where(sel & past, 1.0, 0.0), blk


def _moba_kernel(q_ref, k_ref, v_ref, km_ref, o_ref):
    L = MOBA_BLOCK
    qi = pl.program_id(2)
    q2 = q_ref[0]
    lane = lax.broadcasted_iota(jnp.int32, (1, LANES), 1)
    low = lane < HEAD_DIM
    nblk = km_ref.shape[1]
    km = jnp.concatenate([km_ref[0], jnp.zeros((LANES - nblk, LANES), F32)], axis=0)
    km_hi, km_lo = _split_bf16(km)
    nt = (((1,), (1,)), ((), ()))
    row = lax.broadcasted_iota(jnp.int32, (L, L), 0)
    col = lax.broadcasted_iota(jnp.int32, (L, L), 1)
    causal = col <= row

    own = pl.multiple_of(qi * L, L)
    k_own = k_ref[0, pl.ds(own, L), :]
    v_own = v_ref[0, pl.ds(own, L), :]
    qs, sels, blks, state = [], [], [], []
    for j in range(HEADS_PER_TILE):
        qh = jnp.where(low if j == 0 else ~low, q2, jnp.zeros_like(q2))
        gate = (lax.dot_general(qh, km_hi, nt, preferred_element_type=F32)
                + lax.dot_general(qh, km_lo, nt, preferred_element_type=F32))
        sel, blk = _moba_select(gate, qi)
        s = lax.dot_general(qh, k_own, nt, preferred_element_type=F32)
        s = jnp.where(causal, s, NEG_INF)
        m = jnp.max(s, axis=-1, keepdims=True)
        p = jnp.exp(s - m)
        l = jnp.sum(p, axis=-1, keepdims=True)
        acc = jnp.dot(p.astype(BF16), v_own, preferred_element_type=F32)
        qs.append(qh)
        sels.append(sel)
        blks.append(blk)
        state += [m, l, acc]

    def body(n, carry):
        off = pl.multiple_of(n * L, L)
        k_blk = k_ref[0, pl.ds(off, L), :]
        v_blk = v_ref[0, pl.ds(off, L), :]
        nf = n.astype(F32)
        new = []
        for j in range(HEADS_PER_TILE):
            m, l, acc = carry[3 * j:3 * j + 3]
            chosen = jnp.max(jnp.where(blks[j] == nf, sels[j], 0.0), axis=-1, keepdims=True) > 0.0
            s = lax.dot_general(qs[j], k_blk, nt, preferred_element_type=F32)
            s = jnp.where(chosen, s, NEG_INF)
            m_new = jnp.maximum(m, jnp.max(s, axis=-1, keepdims=True))
            alpha = jnp.exp(m - m_new)
            p = jnp.exp(s - m_new)
            l = alpha * l + jnp.sum(p, axis=-1, keepdims=True)
            acc = alpha * acc + jnp.dot(p.astype(BF16), v_blk, preferred_element_type=F32)
            new += [m_new, l, acc]
        return tuple(new)

    state = lax.fori_loop(0, qi, body, tuple(state))
    o0 = state[2] / state[1]
    o1 = state[5] / state[4]
    o_ref[0] = jnp.where(low, o0, o1).astype(BF16)


def _moba_call(qb, kb, vb, km):
    batch, seq, width = qb.shape
    L = MOBA_BLOCK
    nblk = seq // L
    return pl.pallas_call(
        _moba_kernel,
        grid=(batch, width // LANES, nblk),
        in_specs=[
            pl.BlockSpec((1, L, LANES), lambda b, h, i: (b, i, h)),
            pl.BlockSpec((1, seq, LANES), lambda b, h, i: (b, 0, h)),
            pl.BlockSpec((1, seq, LANES), lambda b, h, i: (b, 0, h)),
            pl.BlockSpec((1, nblk, LANES), lambda b, h, i: (b, 0, h)),
        ],
        out_specs=pl.BlockSpec((1, L, LANES), lambda b, h, i: (b, i, h)),
        out_shape=jax.ShapeDtypeStruct((batch, seq, width), BF16),
        compiler_params=pltpu.CompilerParams(vmem_limit_bytes=VMEM_LIMIT),
        name="moba_attn",
    )(qb, kb, vb, km)


def _post_kernel(x_ref, oa_ref, ob_ref, ga_ref, gb_ref, mod_ref, g_ref,
                 woa_ref, wob_ref, wout_ref, wup_ref, wdn_ref, o_ref):
    ya = jnp.dot(oa_ref[...], woa_ref[...], preferred_element_type=F32)
    yb = jnp.dot(ob_ref[...], wob_ref[...], preferred_element_type=F32)
    mixed = ga_ref[...].astype(F32) * ya + gb_ref[...].astype(F32) * yb
    y = jnp.dot(mixed.astype(BF16), wout_ref[...], preferred_element_type=F32)
    x1 = x_ref[...] + mod_ref[0, 2:3, :] * y
    h = _rms_adaln(x1, g_ref[...], mod_ref[0, 3:4, :], mod_ref[0, 4:5, :]).astype(BF16)
    fc = 1024
    acc = jnp.zeros(x1.shape, F32)
    for c in range(D_FF // fc):
        u = jnp.dot(h, wup_ref[:, c * fc:(c + 1) * fc], preferred_element_type=F32)
        u = jnp.square(jnp.maximum(u, 0.0)).astype(BF16)
        acc += jnp.dot(u, wdn_ref[c * fc:(c + 1) * fc, :], preferred_element_type=F32)
    o_ref[...] = x1 + mod_ref[0, 5:6, :] * acc


def _post_call(x2d, oa, ob, ga, gb, mod, g_mlp, woa, wob, wout, wup, wdn, seq, tm):
    tokens, d = x2d.shape
    steps_per_seq = seq // tm
    row = lambda i: (i, 0)
    const = lambda i: (0, 0)

    def resident(shape):
        return pl.BlockSpec(shape, const, pipeline_mode=pl.Buffered(1))

    return pl.pallas_call(
        _post_kernel,
        grid=(tokens // tm,),
        in_specs=[
            pl.BlockSpec((tm, d), row),
            pl.BlockSpec((tm, SWA_Q_W), row),
            pl.BlockSpec((tm, MOBA_W), row),
            pl.BlockSpec((tm, d), row),
            pl.BlockSpec((tm, d), row),
            pl.BlockSpec((1, 6, d), lambda i: (i // steps_per_seq, 0, 0)),
            pl.BlockSpec((1, d), const),
            resident(woa.shape), resident(wob.shape), resident(wout.shape),
            resident(wup.shape), resident(wdn.shape),
        ],
        out_specs=pl.BlockSpec((tm, d), row),
        out_shape=jax.ShapeDtypeStruct((tokens, d), F32),
        compiler_params=pltpu.CompilerParams(vmem_limit_bytes=VMEM_LIMIT),
        name="post_mlp",
    )(x2d, oa, ob, ga, gb, mod, g_mlp, woa, wob, wout, wup, wdn)


def kernel(x, c, positions, rms_mix, rms_mlp, w_ada, b_ada, w_in, q_norm_swa, k_norm_swa, q_norm_moba,
           k_norm_moba, swa_sinks, w_o_swa, w_o_moba, w_out, w_up, w_down):
    batch, seq, d = x.shape
    depth = w_in.shape[0]
    tokens = batch * seq
    assert seq % MOBA_BLOCK == 0 and d == D_MODEL

    mod = _ada_call(c, w_ada, b_ada).reshape(depth, batch, 6, d)
    cos, sin = _rope_tables(positions)
    cos = cos.reshape(tokens, LANES)
    sin = sin.reshape(tokens, LANES)

    x2d = x.reshape(tokens, d)
    for l in range(depth):
        gains = jnp.stack([jnp.tile(g[l], HEADS_PER_TILE)
                           for g in (q_norm_swa, k_norm_swa, q_norm_moba, k_norm_moba)])
        qa, ka, va, qb, kb, vb, ga, gb, km = _inproj_call(
            x2d, mod[l], rms_mix[l].reshape(1, d), w_in[l].astype(BF16), cos, sin, gains, seq, tm=256)
        nblk = seq // MOBA_BLOCK
        oa = _swa_call(swa_sinks[l], qa.reshape(batch, seq, -1), ka.reshape(batch, seq, -1),
                       va.reshape(batch, seq, -1), tq=256)
        ob = _moba_call(qb.reshape(batch, seq, -1), kb.reshape(batch, seq, -1),
                        vb.reshape(batch, seq, -1), km.reshape(batch, nblk, MOBA_W))
        x2d = _post_call(x2d, oa.reshape(tokens, -1), ob.reshape(tokens, -1), ga, gb, mod[l],
                         rms_mlp[l].reshape(1, d), w_o_swa[l].astype(BF16), w_o_moba[l].astype(BF16),
                         w_out[l].astype(BF16), w_up[l].astype(BF16), w_down[l].astype(BF16), seq, tm=256)
    return x2d.reshape(batch, seq, d)
```

```python
import functools

import jax
import jax.numpy as jnp
import numpy as np
from jax import lax
from jax.experimental import pallas as pl
from jax.experimental.pallas import tpu as pltpu

D_MODEL = 1024
HEAD_DIM = 64
SWA_Q_HEADS = 8
SWA_KV_HEADS = 2
SWA_WINDOW = 128
MOBA_HEADS = 8
MOBA_BLOCK = 256
MOBA_TOPK = 3
D_FF = 4 * D_MODEL
ROPE_THETA = 10000.0
NORM_EPS = 1e-6
NEG_INF = -1e30

SWA_Q_W = SWA_Q_HEADS * HEAD_DIM
SWA_KV_W = SWA_KV_HEADS * HEAD_DIM
MOBA_W = MOBA_HEADS * HEAD_DIM
IN_SPLITS = (SWA_Q_W, SWA_KV_W, SWA_KV_W, MOBA_W, MOBA_W, MOBA_W, D_MODEL, D_MODEL)
IN_W = sum(IN_SPLITS)
IN_OFFSETS = tuple(int(o) for o in np.cumsum((0,) + IN_SPLITS))

LANES = 128
HEADS_PER_TILE = LANES // HEAD_DIM
QK_SCALE = HEAD_DIM ** -0.5
VMEM_LIMIT = 56 * 1024 * 1024

BF16 = jnp.bfloat16
F32 = jnp.float32


def _split_bf16(a):
    hi = a.astype(BF16)
    lo = (a - hi.astype(F32)).astype(BF16)
    return hi, lo


def _ada_kernel(c_ref, w_ref, b_ref, o_ref):
    c = c_ref[...]
    a = c * jax.nn.sigmoid(c)
    a_hi, a_lo = _split_bf16(a)
    w_hi, w_lo = _split_bf16(w_ref[0])
    acc = jnp.dot(a_hi, w_hi, preferred_element_type=F32)
    acc += jnp.dot(a_hi, w_lo, preferred_element_type=F32)
    acc += jnp.dot(a_lo, w_hi, preferred_element_type=F32)
    o_ref[0] = acc + b_ref[0]


def _ada_call(c, w_ada, b_ada):
    depth, d, n = w_ada.shape
    batch = c.shape[0]
    tn = 1024
    return pl.pallas_call(
        _ada_kernel,
        grid=(depth, n // tn),
        in_specs=[
            pl.BlockSpec((batch, d), lambda l, j: (0, 0)),
            pl.BlockSpec((1, d, tn), lambda l, j: (l, 0, j)),
            pl.BlockSpec((1, 1, tn), lambda l, j: (l, 0, j)),
        ],
        out_specs=pl.BlockSpec((1, batch, tn), lambda l, j: (l, 0, j)),
        out_shape=jax.ShapeDtypeStruct((depth, batch, n), F32),
        compiler_params=pltpu.CompilerParams(vmem_limit_bytes=VMEM_LIMIT),
        name="ada_mod",
    )(c, w_ada, b_ada.reshape(depth, 1, n))


def _rope_table_kernel(pos_ref, inv_ref, cos_ref, sin_ref):
    ang = pos_ref[0].astype(F32) * inv_ref[...]
    lane = lax.broadcasted_iota(jnp.int32, ang.shape, 1)
    first_half = (lane % HEAD_DIM) < (HEAD_DIM // 2)
    cos_ref[0] = jnp.cos(ang)
    s = jnp.sin(ang)
    sin_ref[0] = jnp.where(first_half, -s, s)


def _rope_tables(positions):
    batch, seq = positions.shape
    half = HEAD_DIM // 2
    inv_freq = ROPE_THETA ** (-jnp.arange(half, dtype=F32) / half)
    inv_tile = jnp.tile(inv_freq, LANES // half).reshape(1, LANES)
    out = jax.ShapeDtypeStruct((batch, seq, LANES), F32)
    return pl.pallas_call(
        _rope_table_kernel,
        grid=(batch,),
        in_specs=[
            pl.BlockSpec((1, seq, 1), lambda b: (b, 0, 0)),
            pl.BlockSpec((1, LANES), lambda b: (0, 0)),
        ],
        out_specs=[pl.BlockSpec((1, seq, LANES), lambda b: (b, 0, 0))] * 2,
        out_shape=[out, out],
        name="rope_tables",
    )(positions.reshape(batch, seq, 1), inv_tile)


def _rms_adaln(x, gain, shift, scale):
    ms = jnp.mean(x * x, axis=-1, keepdims=True)
    return (x * lax.rsqrt(ms + NORM_EPS) * gain) * (1.0 + scale) + shift


def _head_norm_rope(y, gain, cos, sin, lane):
    low = lane < HEAD_DIM
    y2 = y * y
    s_low = jnp.sum(jnp.where(low, y2, 0.0), axis=-1, keepdims=True)
    s_high = jnp.sum(jnp.where(low, 0.0, y2), axis=-1, keepdims=True)
    ms = jnp.where(low, s_low, s_high) * (1.0 / HEAD_DIM)
    yn = y * lax.rsqrt(ms + NORM_EPS) * gain
    first_half = (lane % HEAD_DIM) < (HEAD_DIM // 2)
    partner = jnp.where(first_half,
                        pltpu.roll(yn, LANES - HEAD_DIM // 2, 1),
                        pltpu.roll(yn, HEAD_DIM // 2, 1))
    return yn * cos + partner * sin


def _inproj_kernel(x_ref, mod_ref, g_ref, w_ref, wvt_ref, cos_ref, sin_ref, hn_ref,
                   qa_ref, ka_ref, va_ref, qb_ref, kb_ref, vbt_ref, ga_ref, gb_ref, km_ref):
    tm = x_ref.shape[0]
    h = _rms_adaln(x_ref[...], g_ref[...], mod_ref[0, 0:1, :], mod_ref[0, 1:2, :])
    hb = h.astype(BF16)
    cos = cos_ref[...]
    sin = sin_ref[...]
    lane = lax.broadcasted_iota(jnp.int32, (1, LANES), 1)
    low = lane < HEAD_DIM

    def proj(i):
        return jnp.dot(hb, w_ref[:, IN_OFFSETS[i]:IN_OFFSETS[i + 1]], preferred_element_type=F32)

    def tiles(y):
        return [y[:, t * LANES:(t + 1) * LANES] for t in range(y.shape[1] // LANES)]

    def both_halves(t):
        r = pltpu.roll(t, HEAD_DIM, 1)
        return jnp.where(low, t, r), jnp.where(low, r, t)

    for t, y in enumerate(tiles(proj(0))):
        o = _head_norm_rope(y, hn_ref[0:1, :], cos, sin, lane) * QK_SCALE
        qa_ref[:, t * LANES:(t + 1) * LANES] = o.astype(BF16)
    k0, k1 = both_halves(_head_norm_rope(proj(1), hn_ref[1:2, :], cos, sin, lane))
    ka_ref[:, 0:LANES] = k0.astype(BF16)
    ka_ref[:, LANES:2 * LANES] = k1.astype(BF16)
    v0, v1 = both_halves(proj(2))
    va_ref[:, 0:LANES] = v0.astype(BF16)
    va_ref[:, LANES:2 * LANES] = v1.astype(BF16)
    for t, y in enumerate(tiles(proj(3))):
        o = _head_norm_rope(y, hn_ref[2:3, :], cos, sin, lane) * QK_SCALE
        qb_ref[:, t * LANES:(t + 1) * LANES] = o.astype(BF16)
    for t, y in enumerate(tiles(proj(4))):
        o = _head_norm_rope(y, hn_ref[3:4, :], cos, sin, lane)
        kb_ref[:, t * LANES:(t + 1) * LANES] = o.astype(BF16)
        for j in range(tm // MOBA_BLOCK):
            blk = o[j * MOBA_BLOCK:(j + 1) * MOBA_BLOCK]
            km_ref[0, j:j + 1, t * LANES:(t + 1) * LANES] = jnp.mean(blk, axis=0, keepdims=True)
    vbt_ref[0] = lax.dot_general(wvt_ref[...], hb, (((1,), (1,)), ((), ())),
                                 preferred_element_type=F32).astype(BF16)
    ga_ref[...] = jax.nn.sigmoid(proj(6)).astype(BF16)
    gb_ref[...] = jax.nn.sigmoid(proj(7)).astype(BF16)


def _inproj_call(x2d, mod, g_mix, w_in, w_vt, cos, sin, head_gains, seq, tm):
    tokens, d = x2d.shape
    batch = tokens // seq
    steps_per_seq = seq // tm
    nkm = tm // MOBA_BLOCK
    row = lambda i: (i, 0)
    const = lambda i: (0, 0)

    def out(width, dtype=BF16):
        return jax.ShapeDtypeStruct((tokens, width), dtype)

    return pl.pallas_call(
        _inproj_kernel,
        grid=(tokens // tm,),
        in_specs=[
            pl.BlockSpec((tm, d), row),
            pl.BlockSpec((1, 6, d), lambda i: (i // steps_per_seq, 0, 0)),
            pl.BlockSpec((1, d), const),
            pl.BlockSpec((d, IN_W), const),
            pl.BlockSpec((MOBA_W, d), const),
            pl.BlockSpec((tm, LANES), row),
            pl.BlockSpec((tm, LANES), row),
            pl.BlockSpec((4, LANES), const),
        ],
        out_specs=[
            pl.BlockSpec((tm, SWA_Q_W), row),
            pl.BlockSpec((tm, 2 * LANES), row),
            pl.BlockSpec((tm, 2 * LANES), row),
            pl.BlockSpec((tm, MOBA_W), row),
            pl.BlockSpec((tm, MOBA_W), row),
            pl.BlockSpec((1, MOBA_W, tm), lambda i: (i // steps_per_seq, 0, i % steps_per_seq)),
            pl.BlockSpec((tm, d), row),
            pl.BlockSpec((tm, d), row),
            pl.BlockSpec((1, nkm, MOBA_W), lambda i: (i, 0, 0)),
        ],
        out_shape=[
            out(SWA_Q_W), out(2 * LANES), out(2 * LANES), out(MOBA_W), out(MOBA_W),
            jax.ShapeDtypeStruct((batch, MOBA_W, seq), BF16),
            out(d), out(d),
            jax.ShapeDtypeStruct((tokens // tm, nkm, MOBA_W), F32),
        ],
        compiler_params=pltpu.CompilerParams(vmem_limit_bytes=VMEM_LIMIT),
        name="in_proj",
    )(x2d, mod, g_mix, w_in, w_vt, cos, sin, head_gains)


def _swa_kernel(sink_ref, q_ref, k_ref, v_ref, o_ref):
    tq = q_ref.shape[1]
    nk = tq + SWA_WINDOW
    qi = pl.program_id(1)
    start = pl.multiple_of(jnp.maximum(qi * tq - SWA_WINDOW, 0), SWA_WINDOW)
    qpos = qi * tq + lax.broadcasted_iota(jnp.int32, (tq, nk), 0)
    kpos = start + lax.broadcasted_iota(jnp.int32, (tq, nk), 1)
    diff = qpos - kpos
    valid = (diff >= 0) & (diff < SWA_WINDOW)
    lane = lax.broadcasted_iota(jnp.int32, (1, LANES), 1)
    low = lane < HEAD_DIM
    group = SWA_Q_HEADS // SWA_KV_HEADS
    for t in range(SWA_Q_HEADS // HEADS_PER_TILE):
        q2 = q_ref[0, :, t * LANES:(t + 1) * LANES]
        kvh = (t * HEADS_PER_TILE) // group
        k2 = k_ref[0, pl.ds(start, nk), kvh * LANES:(kvh + 1) * LANES]
        v2 = v_ref[0, pl.ds(start, nk), kvh * LANES:(kvh + 1) * LANES]
        outs = []
        for j in range(HEADS_PER_TILE):
            sink = sink_ref[t * HEADS_PER_TILE + j]
            qh = jnp.where(low if j == 0 else ~low, q2, jnp.zeros_like(q2))
            s = lax.dot_general(qh, k2, (((1,), (1,)), ((), ())), preferred_element_type=F32)
            s = jnp.where(valid, s, NEG_INF)
            m = jnp.maximum(jnp.max(s, axis=-1, keepdims=True), sink)
            p = jnp.exp(s - m)
            denom = jnp.sum(p, axis=-1, keepdims=True) + jnp.exp(sink - m)
            o = jnp.dot(p.astype(BF16), v2, preferred_element_type=F32)
            outs.append(o / denom)
        o_ref[0, :, t * LANES:(t + 1) * LANES] = jnp.where(low, outs[0], outs[1]).astype(BF16)


def _swa_call(sinks, qa, ka, va, tq):
    batch, seq, _ = qa.shape
    return pl.pallas_call(
        _swa_kernel,
        grid=(batch, seq // tq),
        in_specs=[
            pl.BlockSpec(memory_space=pltpu.SMEM),
            pl.BlockSpec((1, tq, SWA_Q_W), lambda b, i: (b, i, 0)),
            pl.BlockSpec((1, seq, 2 * LANES), lambda b, i: (b, 0, 0)),
            pl.BlockSpec((1, seq, 2 * LANES), lambda b, i: (b, 0, 0)),
        ],
        out_specs=pl.BlockSpec((1, tq, SWA_Q_W), lambda b, i: (b, i, 0)),
        out_shape=jax.ShapeDtypeStruct((batch, seq, SWA_Q_W), BF16),
        compiler_params=pltpu.CompilerParams(vmem_limit_bytes=VMEM_LIMIT),
        name="swa_attn",
    )(sinks, qa, ka, va)


def _moba_bias_t(gate_t, nblk):
    n = lax.broadcasted_iota(jnp.int32, gate_t.shape, 0)
    qblk = lax.broadcasted_iota(jnp.int32, gate_t.shape, 1) // MOBA_BLOCK
    in_range = n < nblk
    past = n < qblk
    nf = n.astype(F32)
    g = jnp.where(past, gate_t, NEG_INF)
    sel = jnp.zeros(gate_t.shape, jnp.bool_)
    for _ in range(MOBA_TOPK):
        top = jnp.max(g, axis=0, keepdims=True)
        idx = jnp.min(jnp.where(g == top, nf, float(LANES)), axis=0, keepdims=True)
        hit = nf == idx
        sel = sel | hit
        g = jnp.where(hit, -jnp.inf, g)
    allowed = (sel & past) | (n == qblk)
    return jnp.where(in_range & ~allowed, NEG_INF, 0.0)


def _moba_kernel(q_ref, k_ref, vt_ref, km_ref, e_ref, o_ref, qa_scr, ka_scr, va_scr, s_scr):
    L = MOBA_BLOCK
    seq = q_ref.shape[1]
    nblk = seq // L
    gate_rows = 16
    lane = lax.broadcasted_iota(jnp.int32, (1, LANES), 1)
    low = lane < HEAD_DIM
    sub = lax.broadcasted_iota(jnp.int32, (LANES, 1), 0)
    top = sub < HEAD_DIM
    nt = (((1,), (1,)), ((), ()))
    causal_t = (lax.broadcasted_iota(jnp.int32, (L, L), 0) <= lax.broadcasted_iota(jnp.int32, (L, L), 1))

    q_all = q_ref[0]
    k2 = k_ref[0]
    vt = vt_ref[0]
    e = e_ref[...]
    km = jnp.concatenate([km_ref[0], jnp.zeros((gate_rows - nblk, LANES), F32)], axis=0)
    km_hi, km_lo = _split_bf16(km)
    zeros = lambda rows: jnp.zeros((rows, seq), F32)

    sum_row = []
    for j in range(HEADS_PER_TILE):
        keep = low if j == 0 else ~low
        keep_rows = top if j == 0 else ~top
        off = HEAD_DIM if j == 0 else 0
        qh = jnp.where(keep, q_all, jnp.zeros_like(q_all))
        gate_t = (lax.dot_general(km_hi, qh, nt, preferred_element_type=F32)
                  + lax.dot_general(km_lo, qh, nt, preferred_element_type=F32))
        bias_t = _moba_bias_t(gate_t, nblk)
        parts = ([zeros(off)] if off else []) + [bias_t, zeros(LANES - off - gate_rows)]
        bias = jnp.concatenate(parts, axis=0).T
        qa_scr[j] = jnp.where(keep, q_all, bias.astype(BF16))
        ka_scr[j] = jnp.where(keep, k2, e)
        va_scr[j] = jnp.where(keep_rows, vt, jnp.ones_like(vt))
        sum_row.append(off)

    units = [(qi, j) for qi in range(nblk) for j in range(HEADS_PER_TILE)]
    col_max, acc, res = {}, {}, {}

    def score_step(u, c):
        qi, j = units[u]
        s = lax.dot_general(ka_scr[j, c * L:(c + 1) * L, :], qa_scr[j, qi * L:(qi + 1) * L, :], nt,
                            preferred_element_type=F32)
        if c == qi:
            s = jnp.where(causal_t, s, NEG_INF)
        s_scr[u % 2, c * L:(c + 1) * L, :] = s
        m = jnp.max(s, axis=0, keepdims=True)
        col_max[u] = m if c == 0 else jnp.maximum(col_max[u], m)

    def value_step(u, c):
        qi, j = units[u]
        p = jnp.exp(s_scr[u % 2, c * L:(c + 1) * L, :] - col_max[u]).astype(BF16)
        part = jnp.dot(va_scr[j, :, c * L:(c + 1) * L], p, preferred_element_type=F32)
        acc[u] = part if c == 0 else acc[u] + part
        if c == qi:
            o = acc.pop(u)
            res[j] = o / o[sum_row[j]:sum_row[j] + 1, :]
            if j == HEADS_PER_TILE - 1:
                o_ref[0, qi * L:(qi + 1) * L, :] = jnp.where(top, res[0], res[1]).T.astype(BF16)

    def steps(fn, u):
        return [functools.partial(fn, u, c) for c in range(units[u][0] + 1)] if u < len(units) else []

    for step in steps(score_step, 0):
        step()
    for u in range(len(units)):
        ahead, behind = steps(score_step, u + 1), steps(value_step, u)
        for i in range(max(len(ahead), len(behind))):
            if i < len(ahead):
                ahead[i]()
            if i < len(behind):
                behind[i]()


def _moba_call(qb, kb, vbt, km):
    batch, seq, width = qb.shape
    nblk = seq // MOBA_BLOCK
    blk_of_key = np.arange(seq)[:, None] // MOBA_BLOCK
    lane = np.arange(LANES)[None, :]
    onehot = jnp.asarray((lane % HEAD_DIM == blk_of_key), dtype=BF16)
    seq_spec = pl.BlockSpec((1, seq, LANES), lambda b, h: (b, 0, h))
    return pl.pallas_call(
        _moba_kernel,
        grid=(batch, width // LANES),
        in_specs=[
            seq_spec, seq_spec,
            pl.BlockSpec((1, LANES, seq), lambda b, h: (b, h, 0)),
            pl.BlockSpec((1, nblk, LANES), lambda b, h: (b, 0, h)),
            pl.BlockSpec((seq, LANES), lambda b, h: (0, 0)),
        ],
        out_specs=seq_spec,
        out_shape=jax.ShapeDtypeStruct((batch, seq, width), BF16),
        scratch_shapes=[
            pltpu.VMEM((HEADS_PER_TILE, seq, LANES), BF16),
            pltpu.VMEM((HEADS_PER_TILE, seq, LANES), BF16),
            pltpu.VMEM((HEADS_PER_TILE, LANES, seq), BF16),
            pltpu.VMEM((2, seq, MOBA_BLOCK), F32),
        ],
        compiler_params=pltpu.CompilerParams(vmem_limit_bytes=VMEM_LIMIT),
        name="moba_attn",
    )(qb, kb, vbt, km, onehot)


def _post_kernel(x_ref, oa_ref, ob_ref, ga_ref, gb_ref, mod_ref, g_ref,
                 woa_ref, wob_ref, wout_ref, wup_ref, wdn_ref, o_ref):
    ya = jnp.dot(oa_ref[...], woa_ref[...], preferred_element_type=F32)
    yb = jnp.dot(ob_ref[...], wob_ref[...], preferred_element_type=F32)
    mixed = ga_ref[...].astype(F32) * ya + gb_ref[...].astype(F32) * yb
    y = jnp.dot(mixed.astype(BF16), wout_ref[...], preferred_element_type=F32)
    x1 = x_ref[...] + mod_ref[0, 2:3, :] * y
    h = _rms_adaln(x1, g_ref[...], mod_ref[0, 3:4, :], mod_ref[0, 4:5, :]).astype(BF16)
    fc = 1024
    acc = jnp.zeros(x1.shape, F32)
    for c in range(D_FF // fc):
        u = jnp.dot(h, wup_ref[:, c * fc:(c + 1) * fc], preferred_element_type=F32)
        u = jnp.square(jnp.maximum(u, 0.0)).astype(BF16)
        acc += jnp.dot(u, wdn_ref[c * fc:(c + 1) * fc, :], preferred_element_type=F32)
    o_ref[...] = x1 + mod_ref[0, 5:6, :] * acc


def _post_call(x2d, oa, ob, ga, gb, mod, g_mlp, woa, wob, wout, wup, wdn, seq, tm):
    tokens, d = x2d.shape
    steps_per_seq = seq // tm
    row = lambda i: (i, 0)
    const = lambda i: (0, 0)

    def resident(shape):
        return pl.BlockSpec(shape, const, pipeline_mode=pl.Buffered(1))

    return pl.pallas_call(
        _post_kernel,
        grid=(tokens // tm,),
        in_specs=[
            pl.BlockSpec((tm, d), row),
            pl.BlockSpec((tm, SWA_Q_W), row),
            pl.BlockSpec((tm, MOBA_W), row),
            pl.BlockSpec((tm, d), row),
            pl.BlockSpec((tm, d), row),
            pl.BlockSpec((1, 6, d), lambda i: (i // steps_per_seq, 0, 0)),
            pl.BlockSpec((1, d), const),
            resident(woa.shape), resident(wob.shape), resident(wout.shape),
            resident(wup.shape), resident(wdn.shape),
        ],
        out_specs=pl.BlockSpec((tm, d), row),
        out_shape=jax.ShapeDtypeStruct((tokens, d), F32),
        compiler_params=pltpu.CompilerParams(vmem_limit_bytes=VMEM_LIMIT),
        name="post_mlp",
    )(x2d, oa, ob, ga, gb, mod, g_mlp, woa, wob, wout, wup, wdn)


def kernel(x, c, positions, rms_mix, rms_mlp, w_ada, b_ada, w_in, q_norm_swa, k_norm_swa, q_norm_moba,
           k_norm_moba, swa_sinks, w_o_swa, w_o_moba, w_out, w_up, w_down):
    batch, seq, d = x.shape
    depth = w_in.shape[0]
    tokens = batch * seq
    assert seq % MOBA_BLOCK == 0 and d == D_MODEL

    mod = _ada_call(c, w_ada, b_ada).reshape(depth, batch, 6, d)
    cos, sin = _rope_tables(positions)
    cos = cos.reshape(tokens, LANES)
    sin = sin.reshape(tokens, LANES)

    x2d = x.reshape(tokens, d)
    for l in range(depth):
        gains = jnp.stack([jnp.tile(g[l], HEADS_PER_TILE)
                           for g in (q_norm_swa, k_norm_swa, q_norm_moba, k_norm_moba)])
        w_in_l = w_in[l].astype(BF16)
        w_vt = w_in_l[:, IN_OFFSETS[5]:IN_OFFSETS[6]].T
        qa, ka, va, qb, kb, vbt, ga, gb, km = _inproj_call(
            x2d, mod[l], rms_mix[l].reshape(1, d), w_in_l, w_vt, cos, sin, gains, seq, tm=256)
        nblk = seq // MOBA_BLOCK
        oa = _swa_call(swa_sinks[l], qa.reshape(batch, seq, -1), ka.reshape(batch, seq, -1),
                       va.reshape(batch, seq, -1), tq=256)
        ob = _moba_call(qb.reshape(batch, seq, -1), kb.reshape(batch, seq, -1), vbt,
                        km.reshape(batch, nblk, MOBA_W))
        x2d = _post_call(x2d, oa.reshape(tokens, -1), ob.reshape(tokens, -1), ga, gb, mod[l],
                         rms_mlp[l].reshape(1, d), w_o_swa[l].astype(BF16), w_o_moba[l].astype(BF16),
                         w_out[l].astype(BF16), w_up[l].astype(BF16), w_down[l].astype(BF16), seq, tm=256)
    return x2d.reshape(batch, seq, d)
```

```python
import functools

import jax
import jax.numpy as jnp
import numpy as np
from jax import lax
from jax.experimental import pallas as pl
from jax.experimental.pallas import tpu as pltpu

D_MODEL = 1024
HEAD_DIM = 64
SWA_Q_HEADS = 8
SWA_KV_HEADS = 2
SWA_WINDOW = 128
MOBA_HEADS = 8
MOBA_BLOCK = 256
MOBA_TOPK = 3
D_FF = 4 * D_MODEL
ROPE_THETA = 10000.0
NORM_EPS = 1e-6
NEG_INF = -1e30

SWA_Q_W = SWA_Q_HEADS * HEAD_DIM
SWA_KV_W = SWA_KV_HEADS * HEAD_DIM
MOBA_W = MOBA_HEADS * HEAD_DIM
IN_SPLITS = (SWA_Q_W, SWA_KV_W, SWA_KV_W, MOBA_W, MOBA_W, MOBA_W, D_MODEL, D_MODEL)
IN_W = sum(IN_SPLITS)
IN_OFFSETS = tuple(int(o) for o in np.cumsum((0,) + IN_SPLITS))

LANES = 128
HEADS_PER_TILE = LANES // HEAD_DIM
QK_SCALE = HEAD_DIM ** -0.5
VMEM_LIMIT = 56 * 1024 * 1024

BF16 = jnp.bfloat16
F32 = jnp.float32


def _split_bf16(a):
    hi = a.astype(BF16)
    lo = (a - hi.astype(F32)).astype(BF16)
    return hi, lo


def _ada_kernel(c_ref, w_ref, b_ref, o_ref):
    c = c_ref[...]
    a = c * jax.nn.sigmoid(c)
    a_hi, a_lo = _split_bf16(a)
    w_hi, w_lo = _split_bf16(w_ref[0])
    acc = jnp.dot(a_hi, w_hi, preferred_element_type=F32)
    acc += jnp.dot(a_hi, w_lo, preferred_element_type=F32)
    acc += jnp.dot(a_lo, w_hi, preferred_element_type=F32)
    o_ref[0] = acc + b_ref[0]


def _ada_call(c, w_ada, b_ada):
    depth, d, n = w_ada.shape
    batch = c.shape[0]
    tn = 1024
    return pl.pallas_call(
        _ada_kernel,
        grid=(depth, n // tn),
        in_specs=[
            pl.BlockSpec((batch, d), lambda l, j: (0, 0)),
            pl.BlockSpec((1, d, tn), lambda l, j: (l, 0, j)),
            pl.BlockSpec((1, 1, tn), lambda l, j: (l, 0, j)),
        ],
        out_specs=pl.BlockSpec((1, batch, tn), lambda l, j: (l, 0, j)),
        out_shape=jax.ShapeDtypeStruct((depth, batch, n), F32),
        compiler_params=pltpu.CompilerParams(vmem_limit_bytes=VMEM_LIMIT),
        name="ada_mod",
    )(c, w_ada, b_ada.reshape(depth, 1, n))


def _rope_table_kernel(pos_ref, inv_ref, cos_ref, sin_ref):
    ang = pos_ref[0].astype(F32) * inv_ref[...]
    lane = lax.broadcasted_iota(jnp.int32, ang.shape, 1)
    first_half = (lane % HEAD_DIM) < (HEAD_DIM // 2)
    cos_ref[0] = jnp.cos(ang)
    s = jnp.sin(ang)
    sin_ref[0] = jnp.where(first_half, -s, s)


def _rope_tables(positions):
    batch, seq = positions.shape
    half = HEAD_DIM // 2
    inv_freq = ROPE_THETA ** (-jnp.arange(half, dtype=F32) / half)
    inv_tile = jnp.tile(inv_freq, LANES // half).reshape(1, LANES)
    out = jax.ShapeDtypeStruct((batch, seq, LANES), F32)
    return pl.pallas_call(
        _rope_table_kernel,
        grid=(batch,),
        in_specs=[
            pl.BlockSpec((1, seq, 1), lambda b: (b, 0, 0)),
            pl.BlockSpec((1, LANES), lambda b: (0, 0)),
        ],
        out_specs=[pl.BlockSpec((1, seq, LANES), lambda b: (b, 0, 0))] * 2,
        out_shape=[out, out],
        name="rope_tables",
    )(positions.reshape(batch, seq, 1), inv_tile)


def _rms_adaln(x, gain, shift, scale):
    ms = jnp.mean(x * x, axis=-1, keepdims=True)
    return (x * lax.rsqrt(ms + NORM_EPS) * gain) * (1.0 + scale) + shift


def _head_norm_rope(y, gain, cos, sin, lane):
    low = lane < HEAD_DIM
    y2 = y * y
    s_low = jnp.sum(jnp.where(low, y2, 0.0), axis=-1, keepdims=True)
    s_high = jnp.sum(jnp.where(low, 0.0, y2), axis=-1, keepdims=True)
    ms = jnp.where(low, s_low, s_high) * (1.0 / HEAD_DIM)
    yn = y * lax.rsqrt(ms + NORM_EPS) * gain
    first_half = (lane % HEAD_DIM) < (HEAD_DIM // 2)
    partner = jnp.where(first_half,
                        pltpu.roll(yn, LANES - HEAD_DIM // 2, 1),
                        pltpu.roll(yn, HEAD_DIM // 2, 1))
    return yn * cos + partner * sin


def _inproj_kernel(x_ref, mod_ref, g_ref, w_ref, wvt_ref, cos_ref, sin_ref, hn_ref,
                   qa_ref, ka_ref, vat_ref, qb_ref, kb_ref, vbt_ref, ga_ref, gb_ref, km_ref):
    tm = x_ref.shape[0]
    h = _rms_adaln(x_ref[...], g_ref[...], mod_ref[0, 0:1, :], mod_ref[0, 1:2, :])
    hb = h.astype(BF16)
    cos = cos_ref[...]
    sin = sin_ref[...]
    lane = lax.broadcasted_iota(jnp.int32, (1, LANES), 1)
    low = lane < HEAD_DIM

    def proj(i):
        return jnp.dot(hb, w_ref[:, IN_OFFSETS[i]:IN_OFFSETS[i + 1]], preferred_element_type=F32)

    def tiles(y):
        return [y[:, t * LANES:(t + 1) * LANES] for t in range(y.shape[1] // LANES)]

    def both_halves(t):
        r = pltpu.roll(t, HEAD_DIM, 1)
        return jnp.where(low, t, r), jnp.where(low, r, t)

    for t, y in enumerate(tiles(proj(0))):
        o = _head_norm_rope(y, hn_ref[0:1, :], cos, sin, lane) * QK_SCALE
        qa_ref[:, t * LANES:(t + 1) * LANES] = o.astype(BF16)
    k0, k1 = both_halves(_head_norm_rope(proj(1), hn_ref[1:2, :], cos, sin, lane))
    ka_ref[:, 0:LANES] = k0.astype(BF16)
    ka_ref[:, LANES:2 * LANES] = k1.astype(BF16)
    for t, y in enumerate(tiles(proj(3))):
        o = _head_norm_rope(y, hn_ref[2:3, :], cos, sin, lane) * QK_SCALE
        qb_ref[:, t * LANES:(t + 1) * LANES] = o.astype(BF16)
    for t, y in enumerate(tiles(proj(4))):
        o = _head_norm_rope(y, hn_ref[3:4, :], cos, sin, lane)
        kb_ref[:, t * LANES:(t + 1) * LANES] = o.astype(BF16)
        for j in range(tm // MOBA_BLOCK):
            blk = o[j * MOBA_BLOCK:(j + 1) * MOBA_BLOCK]
            km_ref[0, j:j + 1, t * LANES:(t + 1) * LANES] = jnp.mean(blk, axis=0, keepdims=True)
    vt = lax.dot_general(wvt_ref[...], hb, (((1,), (1,)), ((), ())), preferred_element_type=F32)
    vat_ref[0] = vt[0:SWA_KV_W].astype(BF16)
    vbt_ref[0] = vt[SWA_KV_W:].astype(BF16)
    ga_ref[...] = jax.nn.sigmoid(proj(6)).astype(BF16)
    gb_ref[...] = jax.nn.sigmoid(proj(7)).astype(BF16)


def _inproj_call(x2d, mod, g_mix, w_in, w_vt, cos, sin, head_gains, seq, tm):
    tokens, d = x2d.shape
    batch = tokens // seq
    steps_per_seq = seq // tm
    nkm = tm // MOBA_BLOCK
    row = lambda i: (i, 0)
    const = lambda i: (0, 0)
    seq_major = lambda i: (i // steps_per_seq, 0, i % steps_per_seq)

    def out(width, dtype=BF16):
        return jax.ShapeDtypeStruct((tokens, width), dtype)

    return pl.pallas_call(
        _inproj_kernel,
        grid=(tokens // tm,),
        in_specs=[
            pl.BlockSpec((tm, d), row),
            pl.BlockSpec((1, 6, d), lambda i: (i // steps_per_seq, 0, 0)),
            pl.BlockSpec((1, d), const),
            pl.BlockSpec((d, IN_W), const),
            pl.BlockSpec((SWA_KV_W + MOBA_W, d), const),
            pl.BlockSpec((tm, LANES), row),
            pl.BlockSpec((tm, LANES), row),
            pl.BlockSpec((4, LANES), const),
        ],
        out_specs=[
            pl.BlockSpec((tm, SWA_Q_W), row),
            pl.BlockSpec((tm, 2 * LANES), row),
            pl.BlockSpec((1, SWA_KV_W, tm), seq_major),
            pl.BlockSpec((tm, MOBA_W), row),
            pl.BlockSpec((tm, MOBA_W), row),
            pl.BlockSpec((1, MOBA_W, tm), seq_major),
            pl.BlockSpec((tm, d), row),
            pl.BlockSpec((tm, d), row),
            pl.BlockSpec((1, nkm, MOBA_W), lambda i: (i, 0, 0)),
        ],
        out_shape=[
            out(SWA_Q_W), out(2 * LANES),
            jax.ShapeDtypeStruct((batch, SWA_KV_W, seq), BF16),
            out(MOBA_W), out(MOBA_W),
            jax.ShapeDtypeStruct((batch, MOBA_W, seq), BF16),
            out(d), out(d),
            jax.ShapeDtypeStruct((tokens // tm, nkm, MOBA_W), F32),
        ],
        compiler_params=pltpu.CompilerParams(vmem_limit_bytes=VMEM_LIMIT),
        name="in_proj",
    )(x2d, mod, g_mix, w_in, w_vt, cos, sin, head_gains)


def _swa_kernel(sink_ref, q_ref, k_ref, vt_ref, o_ref, va_scr, s_scr, mask_scr, *, tq):
    W = SWA_WINDOW
    seq = q_ref.shape[1]
    nk = tq + W
    group = SWA_Q_HEADS // SWA_KV_HEADS
    lane = lax.broadcasted_iota(jnp.int32, (1, LANES), 1)
    low = lane < HEAD_DIM
    sub = lax.broadcasted_iota(jnp.int32, (LANES, 1), 0)
    top = sub < HEAD_DIM
    nt = (((1,), (1,)), ((), ()))
    key = lax.broadcasted_iota(jnp.int32, (nk, tq), 0)
    qry = lax.broadcasted_iota(jnp.int32, (nk, tq), 1)

    for slot, lead in enumerate((0, W)):
        diff = qry - key + lead
        mask_scr[slot] = jnp.where((diff >= 0) & (diff < W), 0.0, NEG_INF)

    ones = jnp.ones((HEAD_DIM, seq), BF16)
    for kvh in range(SWA_KV_HEADS):
        v_kvh = vt_ref[0, kvh * HEAD_DIM:(kvh + 1) * HEAD_DIM, :]
        va_scr[kvh, 0] = jnp.concatenate([v_kvh, ones], axis=0)
        va_scr[kvh, 1] = jnp.concatenate([ones, v_kvh], axis=0)

    units = [(qi, h) for qi in range(seq // tq) for h in range(SWA_Q_HEADS)]
    col_max, res = {}, {}

    def first_key(qi):
        return max(qi * tq - W, 0)

    def score_step(u):
        qi, h = units[u]
        t, j = divmod(h, HEADS_PER_TILE)
        kvh = h // group
        k0 = first_key(qi)
        q2 = q_ref[0, qi * tq:(qi + 1) * tq, t * LANES:(t + 1) * LANES]
        qh = jnp.where(low if j == 0 else ~low, q2, jnp.zeros_like(q2))
        s = lax.dot_general(k_ref[0, k0:k0 + nk, kvh * LANES:(kvh + 1) * LANES], qh, nt,
                            preferred_element_type=F32)
        s = s + mask_scr[min(qi, 1)]
        s_scr[u % 2] = s
        col_max[u] = jnp.maximum(jnp.max(s, axis=0, keepdims=True), sink_ref[h])

    def value_step(u):
        qi, h = units[u]
        t, j = divmod(h, HEADS_PER_TILE)
        kvh = h // group
        k0 = first_key(qi)
        m = col_max.pop(u)
        p = jnp.exp(s_scr[u % 2] - m).astype(BF16)
        o = jnp.dot(va_scr[kvh, j, :, k0:k0 + nk], p, preferred_element_type=F32)
        sum_row = HEAD_DIM if j == 0 else 0
        res[j] = o / (o[sum_row:sum_row + 1, :] + jnp.exp(sink_ref[h] - m))
        if j == HEADS_PER_TILE - 1:
            o_ref[0, qi * tq:(qi + 1) * tq, t * LANES:(t + 1) * LANES] = (
                jnp.where(top, res[0], res[1]).T.astype(BF16))

    score_step(0)
    for u in range(len(units)):
        if u + 1 < len(units):
            score_step(u + 1)
        value_step(u)


def _swa_call(sinks, qa, ka, vat, tq):
    batch, seq, _ = qa.shape
    assert seq % tq == 0 and seq >= tq + SWA_WINDOW and tq % SWA_WINDOW == 0
    return pl.pallas_call(
        functools.partial(_swa_kernel, tq=tq),
        grid=(batch,),
        in_specs=[
            pl.BlockSpec(memory_space=pltpu.SMEM),
            pl.BlockSpec((1, seq, SWA_Q_W), lambda b: (b, 0, 0)),
            pl.BlockSpec((1, seq, 2 * LANES), lambda b: (b, 0, 0)),
            pl.BlockSpec((1, SWA_KV_W, seq), lambda b: (b, 0, 0)),
        ],
        out_specs=pl.BlockSpec((1, seq, SWA_Q_W), lambda b: (b, 0, 0)),
        out_shape=jax.ShapeDtypeStruct((batch, seq, SWA_Q_W), BF16),
        scratch_shapes=[
            pltpu.VMEM((SWA_KV_HEADS, HEADS_PER_TILE, LANES, seq), BF16),
            pltpu.VMEM((2, tq + SWA_WINDOW, tq), F32),
            pltpu.VMEM((2, tq + SWA_WINDOW, tq), F32),
        ],
        compiler_params=pltpu.CompilerParams(vmem_limit_bytes=VMEM_LIMIT),
        name="swa_attn",
    )(sinks, qa, ka, vat)


def _moba_bias_t(gate_t, nblk):
    n = lax.broadcasted_iota(jnp.int32, gate_t.shape, 0)
    qblk = lax.broadcasted_iota(jnp.int32, gate_t.shape, 1) // MOBA_BLOCK
    in_range = n < nblk
    past = n < qblk
    nf = n.astype(F32)
    g = jnp.where(past, gate_t, NEG_INF)
    sel = jnp.zeros(gate_t.shape, jnp.bool_)
    for _ in range(MOBA_TOPK):
        top = jnp.max(g, axis=0, keepdims=True)
        idx = jnp.min(jnp.where(g == top, nf, float(LANES)), axis=0, keepdims=True)
        hit = nf == idx
        sel = sel | hit
        g = jnp.where(hit, -jnp.inf, g)
    allowed = (sel & past) | (n == qblk)
    return jnp.where(in_range & ~allowed, NEG_INF, 0.0)


def _moba_kernel(q_ref, k_ref, vt_ref, km_ref, e_ref, o_ref, qa_scr, ka_scr, va_scr, s_scr):
    L = MOBA_BLOCK
    seq = q_ref.shape[1]
    nblk = seq // L
    gate_rows = 16
    lane = lax.broadcasted_iota(jnp.int32, (1, LANES), 1)
    low = lane < HEAD_DIM
    sub = lax.broadcasted_iota(jnp.int32, (LANES, 1), 0)
    top = sub < HEAD_DIM
    nt = (((1,), (1,)), ((), ()))
    causal_t = (lax.broadcasted_iota(jnp.int32, (L, L), 0) <= lax.broadcasted_iota(jnp.int32, (L, L), 1))

    q_all = q_ref[0]
    k2 = k_ref[0]
    vt = vt_ref[0]
    e = e_ref[...]
    km = jnp.concatenate([km_ref[0], jnp.zeros((gate_rows - nblk, LANES), F32)], axis=0)
    km_hi, km_lo = _split_bf16(km)
    zeros = lambda rows: jnp.zeros((rows, seq), F32)

    sum_row = []
    for j in range(HEADS_PER_TILE):
        keep = low if j == 0 else ~low
        keep_rows = top if j == 0 else ~top
        off = HEAD_DIM if j == 0 else 0
        qh = jnp.where(keep, q_all, jnp.zeros_like(q_all))
        gate_t = (lax.dot_general(km_hi, qh, nt, preferred_element_type=F32)
                  + lax.dot_general(km_lo, qh, nt, preferred_element_type=F32))
        bias_t = _moba_bias_t(gate_t, nblk)
        parts = ([zeros(off)] if off else []) + [bias_t, zeros(LANES - off - gate_rows)]
        bias = jnp.concatenate(parts, axis=0).T
        qa_scr[j] = jnp.where(keep, q_all, bias.astype(BF16))
        ka_scr[j] = jnp.where(keep, k2, e)
        va_scr[j] = jnp.where(keep_rows, vt, jnp.ones_like(vt))
        sum_row.append(off)

    units = [(qi, j) for qi in range(nblk) for j in range(HEADS_PER_TILE)]
    col_max, acc, res = {}, {}, {}

    def score_step(u, c):
        qi, j = units[u]
        s = lax.dot_general(ka_scr[j, c * L:(c + 1) * L, :], qa_scr[j, qi * L:(qi + 1) * L, :], nt,
                            preferred_element_type=F32)
        if c == qi:
            s = jnp.where(causal_t, s, NEG_INF)
        s_scr[u % 2, c * L:(c + 1) * L, :] = s
        m = jnp.max(s, axis=0, keepdims=True)
        col_max[u] = m if c == 0 else jnp.maximum(col_max[u], m)

    def value_step(u, c):
        qi, j = units[u]
        p = jnp.exp(s_scr[u % 2, c * L:(c + 1) * L, :] - col_max[u]).astype(BF16)
        part = jnp.dot(va_scr[j, :, c * L:(c + 1) * L], p, preferred_element_type=F32)
        acc[u] = part if c == 0 else acc[u] + part
        if c == qi:
            o = acc.pop(u)
            res[j] = o / o[sum_row[j]:sum_row[j] + 1, :]
            if j == HEADS_PER_TILE - 1:
                o_ref[0, qi * L:(qi + 1) * L, :] = jnp.where(top, res[0], res[1]).T.astype(BF16)

    def steps(fn, u):
        return [functools.partial(fn, u, c) for c in range(units[u][0] + 1)] if u < len(units) else []

    for step in steps(score_step, 0):
        step()
    for u in range(len(units)):
        ahead, behind = steps(score_step, u + 1), steps(value_step, u)
        for i in range(max(len(ahead), len(behind))):
            if i < len(ahead):
                ahead[i]()
            if i < len(behind):
                behind[i]()


def _moba_call(qb, kb, vbt, km):
    batch, seq, width = qb.shape
    nblk = seq // MOBA_BLOCK
    blk_of_key = np.arange(seq)[:, None] // MOBA_BLOCK
    lane = np.arange(LANES)[None, :]
    onehot = jnp.asarray((lane % HEAD_DIM == blk_of_key), dtype=BF16)
    seq_spec = pl.BlockSpec((1, seq, LANES), lambda b, h: (b, 0, h))
    return pl.pallas_call(
        _moba_kernel,
        grid=(batch, width // LANES),
        in_specs=[
            seq_spec, seq_spec,
            pl.BlockSpec((1, LANES, seq), lambda b, h: (b, h, 0)),
            pl.BlockSpec((1, nblk, LANES), lambda b, h: (b, 0, h)),
            pl.BlockSpec((seq, LANES), lambda b, h: (0, 0)),
        ],
        out_specs=seq_spec,
        out_shape=jax.ShapeDtypeStruct((batch, seq, width), BF16),
        scratch_shapes=[
            pltpu.VMEM((HEADS_PER_TILE, seq, LANES), BF16),
            pltpu.VMEM((HEADS_PER_TILE, seq, LANES), BF16),
            pltpu.VMEM((HEADS_PER_TILE, LANES, seq), BF16),
            pltpu.VMEM((2, seq, MOBA_BLOCK), F32),
        ],
        compiler_params=pltpu.CompilerParams(vmem_limit_bytes=VMEM_LIMIT),
        name="moba_attn",
    )(qb, kb, vbt, km, onehot)


def _post_kernel(x_ref, oa_ref, ob_ref, ga_ref, gb_ref, mod_ref, g_ref,
                 woa_ref, wob_ref, wout_ref, wup_ref, wdn_ref, o_ref):
    ya = jnp.dot(oa_ref[...], woa_ref[...], preferred_element_type=F32)
    yb = jnp.dot(ob_ref[...], wob_ref[...], preferred_element_type=F32)
    mixed = ga_ref[...].astype(F32) * ya + gb_ref[...].astype(F32) * yb
    y = jnp.dot(mixed.astype(BF16), wout_ref[...], preferred_element_type=F32)
    x1 = x_ref[...] + mod_ref[0, 2:3, :] * y
    h = _rms_adaln(x1, g_ref[...], mod_ref[0, 3:4, :], mod_ref[0, 4:5, :]).astype(BF16)
    fc = 1024
    acc = jnp.zeros(x1.shape, F32)
    for c in range(D_FF // fc):
        u = jnp.dot(h, wup_ref[:, c * fc:(c + 1) * fc], preferred_element_type=F32)
        u = jnp.square(jnp.maximum(u, 0.0)).astype(BF16)
        acc += jnp.dot(u, wdn_ref[c * fc:(c + 1) * fc, :], preferred_element_type=F32)
    o_ref[...] = x1 + mod_ref[0, 5:6, :] * acc


def _post_call(x2d, oa, ob, ga, gb, mod, g_mlp, woa, wob, wout, wup, wdn, seq, tm):
    tokens, d = x2d.shape
    steps_per_seq = seq // tm
    row = lambda i: (i, 0)
    const = lambda i: (0, 0)

    def resident(shape):
        return pl.BlockSpec(shape, const, pipeline_mode=pl.Buffered(1))

    return pl.pallas_call(
        _post_kernel,
        grid=(tokens // tm,),
        in_specs=[
            pl.BlockSpec((tm, d), row),
            pl.BlockSpec((tm, SWA_Q_W), row),
            pl.BlockSpec((tm, MOBA_W), row),
            pl.BlockSpec((tm, d), row),
            pl.BlockSpec((tm, d), row),
            pl.BlockSpec((1, 6, d), lambda i: (i // steps_per_seq, 0, 0)),
            pl.BlockSpec((1, d), const),
            resident(woa.shape), resident(wob.shape), resident(wout.shape),
            resident(wup.shape), resident(wdn.shape),
        ],
        out_specs=pl.BlockSpec((tm, d), row),
        out_shape=jax.ShapeDtypeStruct((tokens, d), F32),
        compiler_params=pltpu.CompilerParams(vmem_limit_bytes=VMEM_LIMIT),
        name="post_mlp",
    )(x2d, oa, ob, ga, gb, mod, g_mlp, woa, wob, wout, wup, wdn)


def kernel(x, c, positions, rms_mix, rms_mlp, w_ada, b_ada, w_in, q_norm_swa, k_norm_swa, q_norm_moba,
           k_norm_moba, swa_sinks, w_o_swa, w_o_moba, w_out, w_up, w_down):
    batch, seq, d = x.shape
    depth = w_in.shape[0]
    tokens = batch * seq
    assert seq % MOBA_BLOCK == 0 and d == D_MODEL

    mod = _ada_call(c, w_ada, b_ada).reshape(depth, batch, 6, d)
    cos, sin = _rope_tables(positions)
    cos = cos.reshape(tokens, LANES)
    sin = sin.reshape(tokens, LANES)

    x2d = x.reshape(tokens, d)
    for l in range(depth):
        gains = jnp.stack([jnp.tile(g[l], HEADS_PER_TILE)
                           for g in (q_norm_swa, k_norm_swa, q_norm_moba, k_norm_moba)])
        w_in_l = w_in[l].astype(BF16)
        w_vt = jnp.concatenate([w_in_l[:, IN_OFFSETS[2]:IN_OFFSETS[3]],
                                w_in_l[:, IN_OFFSETS[5]:IN_OFFSETS[6]]], axis=1).T
        qa, ka, vat, qb, kb, vbt, ga, gb, km = _inproj_call(
            x2d, mod[l], rms_mix[l].reshape(1, d), w_in_l, w_vt, cos, sin, gains, seq, tm=256)
        nblk = seq // MOBA_BLOCK
        oa = _swa_call(swa_sinks[l], qa.reshape(batch, seq, -1), ka.reshape(batch, seq, -1), vat, tq=256)
        ob = _moba_call(qb.reshape(batch, seq, -1), kb.reshape(batch, seq, -1), vbt,
                        km.reshape(batch, nblk, MOBA_W))
        x2d = _post_call(x2d, oa.reshape(tokens, -1), ob.reshape(tokens, -1), ga, gb, mod[l],
                         rms_mlp[l].reshape(1, d), w_o_swa[l].astype(BF16), w_o_moba[l].astype(BF16),
                         w_out[l].astype(BF16), w_up[l].astype(BF16), w_down[l].astype(BF16), seq, tm=256)
    return x2d.reshape(batch, seq, d)
```

```python
import functools

import jax
import jax.numpy as jnp
import numpy as np
from jax import lax
from jax.experimental import pallas as pl
from jax.experimental.pallas import tpu as pltpu

D_MODEL = 1024
HEAD_DIM = 64
SWA_Q_HEADS = 8
SWA_KV_HEADS = 2
SWA_WINDOW = 128
MOBA_HEADS = 8
MOBA_BLOCK = 256
MOBA_TOPK = 3
D_FF = 4 * D_MODEL
ROPE_THETA = 10000.0
NORM_EPS = 1e-6
NEG_INF = -1e30

SWA_Q_W = SWA_Q_HEADS * HEAD_DIM
SWA_KV_W = SWA_KV_HEADS * HEAD_DIM
MOBA_W = MOBA_HEADS * HEAD_DIM
IN_SPLITS = (SWA_Q_W, SWA_KV_W, SWA_KV_W, MOBA_W, MOBA_W, MOBA_W, D_MODEL, D_MODEL)
IN_W = sum(IN_SPLITS)
IN_OFFSETS = tuple(int(o) for o in np.cumsum((0,) + IN_SPLITS))

LANES = 128
HEADS_PER_TILE = LANES // HEAD_DIM
LOG2E = 1.4426950408889634
Q_SCALE = HEAD_DIM ** -0.5 * LOG2E
SUM_ROWS = 16
PV_ROWS = HEAD_DIM + SUM_ROWS
VMEM_LIMIT = 56 * 1024 * 1024

BF16 = jnp.bfloat16
F32 = jnp.float32


def _split_bf16(a):
    hi = a.astype(BF16)
    lo = (a - hi.astype(F32)).astype(BF16)
    return hi, lo


def _ada_kernel(c_ref, w_ref, b_ref, o_ref):
    c = c_ref[...]
    a = c * jax.nn.sigmoid(c)
    a_hi, a_lo = _split_bf16(a)
    w_hi, w_lo = _split_bf16(w_ref[0])
    acc = jnp.dot(a_hi, w_hi, preferred_element_type=F32)
    acc += jnp.dot(a_hi, w_lo, preferred_element_type=F32)
    acc += jnp.dot(a_lo, w_hi, preferred_element_type=F32)
    o_ref[0] = acc + b_ref[0]


def _ada_call(c, w_ada, b_ada):
    depth, d, n = w_ada.shape
    batch = c.shape[0]
    tn = 1024
    return pl.pallas_call(
        _ada_kernel,
        grid=(depth, n // tn),
        in_specs=[
            pl.BlockSpec((batch, d), lambda l, j: (0, 0)),
            pl.BlockSpec((1, d, tn), lambda l, j: (l, 0, j)),
            pl.BlockSpec((1, 1, tn), lambda l, j: (l, 0, j)),
        ],
        out_specs=pl.BlockSpec((1, batch, tn), lambda l, j: (l, 0, j)),
        out_shape=jax.ShapeDtypeStruct((depth, batch, n), F32),
        compiler_params=pltpu.CompilerParams(vmem_limit_bytes=VMEM_LIMIT),
        name="ada_mod",
    )(c, w_ada, b_ada.reshape(depth, 1, n))


def _rope_table_kernel(pos_ref, inv_ref, cos_ref, sin_ref):
    ang = pos_ref[0].astype(F32) * inv_ref[...]
    lane = lax.broadcasted_iota(jnp.int32, ang.shape, 1)
    first_half = (lane % HEAD_DIM) < (HEAD_DIM // 2)
    cos_ref[0] = jnp.cos(ang)
    s = jnp.sin(ang)
    sin_ref[0] = jnp.where(first_half, -s, s)


def _rope_tables(positions):
    batch, seq = positions.shape
    half = HEAD_DIM // 2
    inv_freq = ROPE_THETA ** (-jnp.arange(half, dtype=F32) / half)
    inv_tile = jnp.tile(inv_freq, LANES // half).reshape(1, LANES)
    out = jax.ShapeDtypeStruct((batch, seq, LANES), F32)
    return pl.pallas_call(
        _rope_table_kernel,
        grid=(batch,),
        in_specs=[
            pl.BlockSpec((1, seq, 1), lambda b: (b, 0, 0)),
            pl.BlockSpec((1, LANES), lambda b: (0, 0)),
        ],
        out_specs=[pl.BlockSpec((1, seq, LANES), lambda b: (b, 0, 0))] * 2,
        out_shape=[out, out],
        name="rope_tables",
    )(positions.reshape(batch, seq, 1), inv_tile)


def _rms_adaln(x, gain, shift, scale):
    ms = jnp.mean(x * x, axis=-1, keepdims=True)
    return (x * lax.rsqrt(ms + NORM_EPS) * gain) * (1.0 + scale) + shift


def _head_norm_rope(y, gain, cos, sin, lane):
    low = lane < HEAD_DIM
    y2 = y * y
    s_low = jnp.sum(jnp.where(low, y2, 0.0), axis=-1, keepdims=True)
    s_high = jnp.sum(jnp.where(low, 0.0, y2), axis=-1, keepdims=True)
    ms = jnp.where(low, s_low, s_high) * (1.0 / HEAD_DIM)
    yn = y * lax.rsqrt(ms + NORM_EPS) * gain
    first_half = (lane % HEAD_DIM) < (HEAD_DIM // 2)
    partner = jnp.where(first_half,
                        pltpu.roll(yn, LANES - HEAD_DIM // 2, 1),
                        pltpu.roll(yn, HEAD_DIM // 2, 1))
    return yn * cos + partner * sin


def _inproj_kernel(x_ref, mod_ref, g_ref, w_ref, wvt_ref, cos_ref, sin_ref, hn_ref,
                   qa_ref, ka_ref, vat_ref, qb_ref, kb_ref, vbt_ref, ga_ref, gb_ref, km_ref):
    tm = x_ref.shape[0]
    h = _rms_adaln(x_ref[...], g_ref[...], mod_ref[0, 0:1, :], mod_ref[0, 1:2, :])
    hb = h.astype(BF16)
    cos = cos_ref[...]
    sin = sin_ref[...]
    lane = lax.broadcasted_iota(jnp.int32, (1, LANES), 1)
    low = lane < HEAD_DIM

    def proj(i):
        return jnp.dot(hb, w_ref[:, IN_OFFSETS[i]:IN_OFFSETS[i + 1]], preferred_element_type=F32)

    def tiles(y):
        return [y[:, t * LANES:(t + 1) * LANES] for t in range(y.shape[1] // LANES)]

    def both_halves(t):
        r = pltpu.roll(t, HEAD_DIM, 1)
        return jnp.where(low, t, r), jnp.where(low, r, t)

    for t, y in enumerate(tiles(proj(0))):
        o = _head_norm_rope(y, hn_ref[0:1, :], cos, sin, lane) * Q_SCALE
        qa_ref[:, t * LANES:(t + 1) * LANES] = o.astype(BF16)
    k0, k1 = both_halves(_head_norm_rope(proj(1), hn_ref[1:2, :], cos, sin, lane))
    ka_ref[:, 0:LANES] = k0.astype(BF16)
    ka_ref[:, LANES:2 * LANES] = k1.astype(BF16)
    for t, y in enumerate(tiles(proj(3))):
        o = _head_norm_rope(y, hn_ref[2:3, :], cos, sin, lane) * Q_SCALE
        qb_ref[:, t * LANES:(t + 1) * LANES] = o.astype(BF16)
    for t, y in enumerate(tiles(proj(4))):
        o = _head_norm_rope(y, hn_ref[3:4, :], cos, sin, lane)
        kb_ref[:, t * LANES:(t + 1) * LANES] = o.astype(BF16)
        for j in range(tm // MOBA_BLOCK):
            blk = o[j * MOBA_BLOCK:(j + 1) * MOBA_BLOCK]
            km_ref[0, j:j + 1, t * LANES:(t + 1) * LANES] = jnp.mean(blk, axis=0, keepdims=True)
    vt = lax.dot_general(wvt_ref[...], hb, (((1,), (1,)), ((), ())), preferred_element_type=F32)
    vat_ref[0] = vt[0:SWA_KV_W].astype(BF16)
    vbt_ref[0] = vt[SWA_KV_W:].astype(BF16)
    ga_ref[...] = jax.nn.sigmoid(proj(6)).astype(BF16)
    gb_ref[...] = jax.nn.sigmoid(proj(7)).astype(BF16)


def _inproj_call(x2d, mod, g_mix, w_in, w_vt, cos, sin, head_gains, seq, tm):
    tokens, d = x2d.shape
    batch = tokens // seq
    steps_per_seq = seq // tm
    nkm = tm // MOBA_BLOCK
    row = lambda i: (i, 0)
    const = lambda i: (0, 0)
    seq_major = lambda i: (i // steps_per_seq, 0, i % steps_per_seq)

    def out(width, dtype=BF16):
        return jax.ShapeDtypeStruct((tokens, width), dtype)

    return pl.pallas_call(
        _inproj_kernel,
        grid=(tokens // tm,),
        in_specs=[
            pl.BlockSpec((tm, d), row),
            pl.BlockSpec((1, 6, d), lambda i: (i // steps_per_seq, 0, 0)),
            pl.BlockSpec((1, d), const),
            pl.BlockSpec((d, IN_W), const),
            pl.BlockSpec((SWA_KV_W + MOBA_W, d), const),
            pl.BlockSpec((tm, LANES), row),
            pl.BlockSpec((tm, LANES), row),
            pl.BlockSpec((4, LANES), const),
        ],
        out_specs=[
            pl.BlockSpec((tm, SWA_Q_W), row),
            pl.BlockSpec((tm, 2 * LANES), row),
            pl.BlockSpec((1, SWA_KV_W, tm), seq_major),
            pl.BlockSpec((tm, MOBA_W), row),
            pl.BlockSpec((tm, MOBA_W), row),
            pl.BlockSpec((1, MOBA_W, tm), seq_major),
            pl.BlockSpec((tm, d), row),
            pl.BlockSpec((tm, d), row),
            pl.BlockSpec((1, nkm, MOBA_W), lambda i: (i, 0, 0)),
        ],
        out_shape=[
            out(SWA_Q_W), out(2 * LANES),
            jax.ShapeDtypeStruct((batch, SWA_KV_W, seq), BF16),
            out(MOBA_W), out(MOBA_W),
            jax.ShapeDtypeStruct((batch, MOBA_W, seq), BF16),
            out(d), out(d),
            jax.ShapeDtypeStruct((tokens // tm, nkm, MOBA_W), F32),
        ],
        compiler_params=pltpu.CompilerParams(vmem_limit_bytes=VMEM_LIMIT),
        name="in_proj",
    )(x2d, mod, g_mix, w_in, w_vt, cos, sin, head_gains)


def _swa_kernel(sink_ref, q_ref, k_ref, vt_ref, o_ref, va_scr, s_scr, p_scr, mask_scr, *, tq):
    W = SWA_WINDOW
    seq = q_ref.shape[1]
    nk = tq + W
    group = SWA_Q_HEADS // SWA_KV_HEADS
    lane = lax.broadcasted_iota(jnp.int32, (1, LANES), 1)
    low = lane < HEAD_DIM
    nt = (((1,), (1,)), ((), ()))
    key = lax.broadcasted_iota(jnp.int32, (nk, tq), 0)
    qry = lax.broadcasted_iota(jnp.int32, (nk, tq), 1)

    for slot, lead in enumerate((0, W)):
        diff = qry - key + lead
        mask_scr[slot] = jnp.where((diff >= 0) & (diff < W), 0.0, NEG_INF)

    ones = jnp.ones((SUM_ROWS, seq), BF16)
    for kvh in range(SWA_KV_HEADS):
        va_scr[kvh] = jnp.concatenate([vt_ref[0, kvh * HEAD_DIM:(kvh + 1) * HEAD_DIM, :], ones], axis=0)

    units = [(qi, h) for qi in range(seq // tq) for h in range(SWA_Q_HEADS)]
    col_max, res = {}, {}

    def first_key(qi):
        return max(qi * tq - W, 0)

    def score_step(u):
        qi, h = units[u]
        t, j = divmod(h, HEADS_PER_TILE)
        kvh = h // group
        k0 = first_key(qi)
        q2 = q_ref[0, qi * tq:(qi + 1) * tq, t * LANES:(t + 1) * LANES]
        qh = jnp.where(low if j == 0 else ~low, q2, jnp.zeros_like(q2))
        s = lax.dot_general(k_ref[0, k0:k0 + nk, kvh * LANES:(kvh + 1) * LANES], qh, nt,
                            preferred_element_type=F32)
        s = s + mask_scr[min(qi, 1)]
        s_scr[u % 2] = s
        col_max[u] = jnp.maximum(jnp.max(s, axis=0, keepdims=True), sink_ref[h] * LOG2E)

    def exp_step(u):
        p_scr[u % 2] = jnp.exp2(s_scr[u % 2] - col_max[u]).astype(BF16)

    def value_step(u):
        qi, h = units[u]
        t, j = divmod(h, HEADS_PER_TILE)
        k0 = first_key(qi)
        o = jnp.dot(va_scr[h // group, :, k0:k0 + nk], p_scr[u % 2], preferred_element_type=F32)
        denom = o[HEAD_DIM:HEAD_DIM + 1, :] + jnp.exp2(sink_ref[h] * LOG2E - col_max.pop(u))
        res[j] = o[0:HEAD_DIM] / denom
        if j == HEADS_PER_TILE - 1:
            o_ref[0, qi * tq:(qi + 1) * tq, t * LANES:(t + 1) * LANES] = (
                jnp.concatenate([res[0], res[1]], axis=0).T.astype(BF16))

    for r in range(len(units) + 2):
        for fn, u in ((score_step, r), (exp_step, r - 1), (value_step, r - 2)):
            if 0 <= u < len(units):
                fn(u)


def _swa_call(sinks, qa, ka, vat, tq):
    batch, seq, _ = qa.shape
    assert seq % tq == 0 and seq >= tq + SWA_WINDOW and tq % SWA_WINDOW == 0
    return pl.pallas_call(
        functools.partial(_swa_kernel, tq=tq),
        grid=(batch,),
        in_specs=[
            pl.BlockSpec(memory_space=pltpu.SMEM),
            pl.BlockSpec((1, seq, SWA_Q_W), lambda b: (b, 0, 0)),
            pl.BlockSpec((1, seq, 2 * LANES), lambda b: (b, 0, 0)),
            pl.BlockSpec((1, SWA_KV_W, seq), lambda b: (b, 0, 0)),
        ],
        out_specs=pl.BlockSpec((1, seq, SWA_Q_W), lambda b: (b, 0, 0)),
        out_shape=jax.ShapeDtypeStruct((batch, seq, SWA_Q_W), BF16),
        scratch_shapes=[
            pltpu.VMEM((SWA_KV_HEADS, PV_ROWS, seq), BF16),
            pltpu.VMEM((2, tq + SWA_WINDOW, tq), F32),
            pltpu.VMEM((2, tq + SWA_WINDOW, tq), BF16),
            pltpu.VMEM((2, tq + SWA_WINDOW, tq), F32),
        ],
        compiler_params=pltpu.CompilerParams(vmem_limit_bytes=VMEM_LIMIT),
        name="swa_attn",
    )(sinks, qa, ka, vat)


def _moba_bias_t(gate_t, nblk):
    n = lax.broadcasted_iota(jnp.int32, gate_t.shape, 0)
    qblk = lax.broadcasted_iota(jnp.int32, gate_t.shape, 1) // MOBA_BLOCK
    in_range = n < nblk
    past = n < qblk
    nf = n.astype(F32)
    g = jnp.where(past, gate_t, NEG_INF)
    sel = jnp.zeros(gate_t.shape, jnp.bool_)
    for _ in range(MOBA_TOPK):
        top = jnp.max(g, axis=0, keepdims=True)
        idx = jnp.min(jnp.where(g == top, nf, float(LANES)), axis=0, keepdims=True)
        hit = nf == idx
        sel = sel | hit
        g = jnp.where(hit, -jnp.inf, g)
    allowed = (sel & past) | (n == qblk)
    return jnp.where(in_range & ~allowed, NEG_INF, 0.0)


def _moba_kernel(q_ref, k_ref, vt_ref, km_ref, e_ref, o_ref, qa_scr, ka_scr, va_scr, s_scr, p_scr):
    L = MOBA_BLOCK
    seq = q_ref.shape[1]
    nblk = seq // L
    gate_rows = 16
    lane = lax.broadcasted_iota(jnp.int32, (1, LANES), 1)
    low = lane < HEAD_DIM
    nt = (((1,), (1,)), ((), ()))
    causal_t = (lax.broadcasted_iota(jnp.int32, (L, L), 0) <= lax.broadcasted_iota(jnp.int32, (L, L), 1))

    q_all = q_ref[0]
    k2 = k_ref[0]
    vt = vt_ref[0]
    e = e_ref[...]
    km = jnp.concatenate([km_ref[0], jnp.zeros((gate_rows - nblk, LANES), F32)], axis=0)
    km_hi, km_lo = _split_bf16(km)
    zeros = lambda rows: jnp.zeros((rows, seq), F32)

    ones = jnp.ones((SUM_ROWS, seq), BF16)
    for j in range(HEADS_PER_TILE):
        keep = low if j == 0 else ~low
        off = HEAD_DIM if j == 0 else 0
        qh = jnp.where(keep, q_all, jnp.zeros_like(q_all))
        gate_t = (lax.dot_general(km_hi, qh, nt, preferred_element_type=F32)
                  + lax.dot_general(km_lo, qh, nt, preferred_element_type=F32))
        bias_t = _moba_bias_t(gate_t, nblk)
        parts = ([zeros(off)] if off else []) + [bias_t, zeros(LANES - off - gate_rows)]
        bias = jnp.concatenate(parts, axis=0).T
        qa_scr[j] = jnp.where(keep, q_all, bias.astype(BF16))
        ka_scr[j] = jnp.where(keep, k2, e)
        va_scr[j] = jnp.concatenate([vt[j * HEAD_DIM:(j + 1) * HEAD_DIM], ones], axis=0)

    units = [(qi, j) for qi in range(nblk) for j in range(HEADS_PER_TILE)]
    col_max, acc, res = {}, {}, {}

    def score_step(u, c):
        qi, j = units[u]
        s = lax.dot_general(ka_scr[j, c * L:(c + 1) * L, :], qa_scr[j, qi * L:(qi + 1) * L, :], nt,
                            preferred_element_type=F32)
        if c == qi:
            s = jnp.where(causal_t, s, NEG_INF)
        s_scr[u % 2, c * L:(c + 1) * L, :] = s
        m = jnp.max(s, axis=0, keepdims=True)
        col_max[u] = m if c == 0 else jnp.maximum(col_max[u], m)

    def exp_step(u, c):
        p_scr[u % 2, c * L:(c + 1) * L, :] = jnp.exp2(
            s_scr[u % 2, c * L:(c + 1) * L, :] - col_max[u]).astype(BF16)

    def value_step(u, c):
        qi, j = units[u]
        part = jnp.dot(va_scr[j, :, c * L:(c + 1) * L], p_scr[u % 2, c * L:(c + 1) * L, :],
                       preferred_element_type=F32)
        acc[u] = part if c == 0 else acc[u] + part
        if c == qi:
            o = acc.pop(u)
            res[j] = o[0:HEAD_DIM] / o[HEAD_DIM:HEAD_DIM + 1, :]
            if j == HEADS_PER_TILE - 1:
                o_ref[0, qi * L:(qi + 1) * L, :] = jnp.concatenate([res[0], res[1]], axis=0).T.astype(BF16)

    def steps(fn, u):
        if not 0 <= u < len(units):
            return []
        return [functools.partial(fn, u, c) for c in range(units[u][0] + 1)]

    for r in range(len(units) + 2):
        stages = [steps(score_step, r), steps(exp_step, r - 1), steps(value_step, r - 2)]
        for i in range(max(len(s) for s in stages)):
            for stage in stages:
                if i < len(stage):
                    stage[i]()


def _moba_call(qb, kb, vbt, km):
    batch, seq, width = qb.shape
    nblk = seq // MOBA_BLOCK
    blk_of_key = np.arange(seq)[:, None] // MOBA_BLOCK
    lane = np.arange(LANES)[None, :]
    onehot = jnp.asarray((lane % HEAD_DIM == blk_of_key), dtype=BF16)
    seq_spec = pl.BlockSpec((1, seq, LANES), lambda b, h: (b, 0, h))
    return pl.pallas_call(
        _moba_kernel,
        grid=(batch, width // LANES),
        in_specs=[
            seq_spec, seq_spec,
            pl.BlockSpec((1, LANES, seq), lambda b, h: (b, h, 0)),
            pl.BlockSpec((1, nblk, LANES), lambda b, h: (b, 0, h)),
            pl.BlockSpec((seq, LANES), lambda b, h: (0, 0)),
        ],
        out_specs=seq_spec,
        out_shape=jax.ShapeDtypeStruct((batch, seq, width), BF16),
        scratch_shapes=[
            pltpu.VMEM((HEADS_PER_TILE, seq, LANES), BF16),
            pltpu.VMEM((HEADS_PER_TILE, seq, LANES), BF16),
            pltpu.VMEM((HEADS_PER_TILE, PV_ROWS, seq), BF16),
            pltpu.VMEM((2, seq, MOBA_BLOCK), F32),
            pltpu.VMEM((2, seq, MOBA_BLOCK), BF16),
        ],
        compiler_params=pltpu.CompilerParams(vmem_limit_bytes=VMEM_LIMIT),
        name="moba_attn",
    )(qb, kb, vbt, km, onehot)


def _post_kernel(x_ref, oa_ref, ob_ref, ga_ref, gb_ref, mod_ref, g_ref,
                 woa_ref, wob_ref, wout_ref, wup_ref, wdn_ref, o_ref):
    ya = jnp.dot(oa_ref[...], woa_ref[...], preferred_element_type=F32)
    yb = jnp.dot(ob_ref[...], wob_ref[...], preferred_element_type=F32)
    mixed = ga_ref[...].astype(F32) * ya + gb_ref[...].astype(F32) * yb
    y = jnp.dot(mixed.astype(BF16), wout_ref[...], preferred_element_type=F32)
    x1 = x_ref[...] + mod_ref[0, 2:3, :] * y
    h = _rms_adaln(x1, g_ref[...], mod_ref[0, 3:4, :], mod_ref[0, 4:5, :]).astype(BF16)
    fc = 1024
    acc = jnp.zeros(x1.shape, F32)
    for c in range(D_FF // fc):
        u = jnp.dot(h, wup_ref[:, c * fc:(c + 1) * fc], preferred_element_type=F32)
        u = jnp.square(jnp.maximum(u, 0.0)).astype(BF16)
        acc += jnp.dot(u, wdn_ref[c * fc:(c + 1) * fc, :], preferred_element_type=F32)
    o_ref[...] = x1 + mod_ref[0, 5:6, :] * acc


def _post_call(x2d, oa, ob, ga, gb, mod, g_mlp, woa, wob, wout, wup, wdn, seq, tm):
    tokens, d = x2d.shape
    steps_per_seq = seq // tm
    row = lambda i: (i, 0)
    const = lambda i: (0, 0)

    def resident(shape):
        return pl.BlockSpec(shape, const, pipeline_mode=pl.Buffered(1))

    return pl.pallas_call(
        _post_kernel,
        grid=(tokens // tm,),
        in_specs=[
            pl.BlockSpec((tm, d), row),
            pl.BlockSpec((tm, SWA_Q_W), row),
            pl.BlockSpec((tm, MOBA_W), row),
            pl.BlockSpec((tm, d), row),
            pl.BlockSpec((tm, d), row),
            pl.BlockSpec((1, 6, d), lambda i: (i // steps_per_seq, 0, 0)),
            pl.BlockSpec((1, d), const),
            resident(woa.shape), resident(wob.shape), resident(wout.shape),
            resident(wup.shape), resident(wdn.shape),
        ],
        out_specs=pl.BlockSpec((tm, d), row),
        out_shape=jax.ShapeDtypeStruct((tokens, d), F32),
        compiler_params=pltpu.CompilerParams(vmem_limit_bytes=VMEM_LIMIT),
        name="post_mlp",
    )(x2d, oa, ob, ga, gb, mod, g_mlp, woa, wob, wout, wup, wdn)


def kernel(x, c, positions, rms_mix, rms_mlp, w_ada, b_ada, w_in, q_norm_swa, k_norm_swa, q_norm_moba,
           k_norm_moba, swa_sinks, w_o_swa, w_o_moba, w_out, w_up, w_down):
    batch, seq, d = x.shape
    depth = w_in.shape[0]
    tokens = batch * seq
    assert seq % MOBA_BLOCK == 0 and d == D_MODEL

    mod = _ada_call(c, w_ada, b_ada).reshape(depth, batch, 6, d)
    cos, sin = _rope_tables(positions)
    cos = cos.reshape(tokens, LANES)
    sin = sin.reshape(tokens, LANES)

    x2d = x.reshape(tokens, d)
    for l in range(depth):
        gains = jnp.stack([jnp.tile(g[l], HEADS_PER_TILE)
                           for g in (q_norm_swa, k_norm_swa, q_norm_moba, k_norm_moba)])
        w_in_l = w_in[l].astype(BF16)
        w_vt = jnp.concatenate([w_in_l[:, IN_OFFSETS[2]:IN_OFFSETS[3]],
                                w_in_l[:, IN_OFFSETS[5]:IN_OFFSETS[6]]], axis=1).T
        qa, ka, vat, qb, kb, vbt, ga, gb, km = _inproj_call(
            x2d, mod[l], rms_mix[l].reshape(1, d), w_in_l, w_vt, cos, sin, gains, seq, tm=512)
        nblk = seq // MOBA_BLOCK
        oa = _swa_call(swa_sinks[l], qa.reshape(batch, seq, -1), ka.reshape(batch, seq, -1), vat, tq=256)
        ob = _moba_call(qb.reshape(batch, seq, -1), kb.reshape(batch, seq, -1), vbt,
                        km.reshape(batch, nblk, MOBA_W))
        x2d = _post_call(x2d, oa.reshape(tokens, -1), ob.reshape(tokens, -1), ga, gb, mod[l],
                         rms_mlp[l].reshape(1, d), w_o_swa[l].astype(BF16), w_o_moba[l].astype(BF16),
                         w_out[l].astype(BF16), w_up[l].astype(BF16), w_down[l].astype(BF16), seq, tm=512)
    return x2d.reshape(batch, seq, d)
```

```python
import functools

import jax
import jax.numpy as jnp
import numpy as np
from jax import lax
from jax.experimental import pallas as pl
from jax.experimental.pallas import tpu as pltpu

D_MODEL = 1024
HEAD_DIM = 64
SWA_Q_HEADS = 8
SWA_KV_HEADS = 2
SWA_WINDOW = 128
MOBA_HEADS = 8
MOBA_BLOCK = 256
MOBA_TOPK = 3
D_FF = 4 * D_MODEL
ROPE_THETA = 10000.0
NORM_EPS = 1e-6
NEG_INF = -1e30

SWA_Q_W = SWA_Q_HEADS * HEAD_DIM
SWA_KV_W = SWA_KV_HEADS * HEAD_DIM
MOBA_W = MOBA_HEADS * HEAD_DIM
IN_SPLITS = (SWA_Q_W, SWA_KV_W, SWA_KV_W, MOBA_W, MOBA_W, MOBA_W, D_MODEL, D_MODEL)
IN_W = sum(IN_SPLITS)
IN_OFFSETS = tuple(int(o) for o in np.cumsum((0,) + IN_SPLITS))

LANES = 128
HEADS_PER_TILE = LANES // HEAD_DIM
LOG2E = 1.4426950408889634
Q_SCALE = HEAD_DIM ** -0.5 * LOG2E
SUM_ROWS = 16
PV_ROWS = HEAD_DIM + SUM_ROWS
VMEM_LIMIT = 56 * 1024 * 1024
ROW_TILE = 512
SWA_Q_TILE = 256
FF_CHUNK = 1024

BF16 = jnp.bfloat16
F32 = jnp.float32


def _split_bf16(a):
    hi = a.astype(BF16)
    lo = (a - hi.astype(F32)).astype(BF16)
    return hi, lo


def _ada_kernel(c_ref, w_ref, b_ref, o_ref):
    c = c_ref[...]
    a = c * jax.nn.sigmoid(c)
    a_hi, a_lo = _split_bf16(a)
    w_hi, w_lo = _split_bf16(w_ref[0])
    acc = jnp.dot(a_hi, w_hi, preferred_element_type=F32)
    acc += jnp.dot(a_hi, w_lo, preferred_element_type=F32)
    acc += jnp.dot(a_lo, w_hi, preferred_element_type=F32)
    o_ref[0] = acc + b_ref[0]


def _ada_call(c, w_ada, b_ada):
    depth, d, n = w_ada.shape
    batch = c.shape[0]
    tn = 1024
    return pl.pallas_call(
        _ada_kernel,
        grid=(depth, n // tn),
        in_specs=[
            pl.BlockSpec((batch, d), lambda l, j: (0, 0)),
            pl.BlockSpec((1, d, tn), lambda l, j: (l, 0, j)),
            pl.BlockSpec((1, 1, tn), lambda l, j: (l, 0, j)),
        ],
        out_specs=pl.BlockSpec((1, batch, tn), lambda l, j: (l, 0, j)),
        out_shape=jax.ShapeDtypeStruct((depth, batch, n), F32),
        compiler_params=pltpu.CompilerParams(vmem_limit_bytes=VMEM_LIMIT),
        name="ada_mod",
    )(c, w_ada, b_ada.reshape(depth, 1, n))


def _rope_table_kernel(pos_ref, inv_ref, cos_ref, sin_ref):
    ang = pos_ref[0] * inv_ref[...]
    cos_ref[0] = jnp.cos(ang)
    sin_ref[0] = jnp.sin(ang)


def _rope_tables(positions):
    batch, seq = positions.shape
    half = HEAD_DIM // 2
    per_row = LANES // half
    inv_freq = ROPE_THETA ** (-jnp.arange(half, dtype=F32) / half)
    inv_tile = jnp.tile(inv_freq, per_row).reshape(1, LANES)
    pos = jnp.repeat(positions.astype(F32).reshape(batch, seq // per_row, per_row), half, axis=2)
    spec = pl.BlockSpec((1, seq // per_row, LANES), lambda b: (b, 0, 0))
    out = jax.ShapeDtypeStruct((batch, seq // per_row, LANES), F32)
    cos, sin = pl.pallas_call(
        _rope_table_kernel,
        grid=(batch,),
        in_specs=[spec, pl.BlockSpec((1, LANES), lambda b: (0, 0))],
        out_specs=[spec, spec],
        out_shape=[out, out],
        name="rope_tables",
    )(pos, inv_tile)
    cos = cos.reshape(batch * seq, half)
    sin = sin.reshape(batch * seq, half)
    return jnp.tile(cos, (1, per_row)), jnp.concatenate([-sin, sin] * (per_row // 2), axis=1)


def _rms_adaln(x, gain, shift, scale):
    ms = jnp.mean(x * x, axis=-1, keepdims=True)
    return (x * lax.rsqrt(ms + NORM_EPS) * gain) * (1.0 + scale) + shift


def _head_norm_rope(y, gain, cos, sin, lane):
    low = lane < HEAD_DIM
    y2 = y * y
    s_low = jnp.sum(jnp.where(low, y2, 0.0), axis=-1, keepdims=True)
    s_high = jnp.sum(jnp.where(low, 0.0, y2), axis=-1, keepdims=True)
    ms = jnp.where(low, s_low, s_high) * (1.0 / HEAD_DIM)
    yn = y * lax.rsqrt(ms + NORM_EPS) * gain
    first_half = (lane % HEAD_DIM) < (HEAD_DIM // 2)
    partner = jnp.where(first_half,
                        pltpu.roll(yn, LANES - HEAD_DIM // 2, 1),
                        pltpu.roll(yn, HEAD_DIM // 2, 1))
    return yn * cos + partner * sin


def _inproj_kernel(x_ref, mod_ref, g_ref, w_ref, cos_ref, sin_ref, hn_ref,
                   qa_ref, ka_ref, vat_ref, qb_ref, kb_ref, vbt_ref, ga_ref, gb_ref, km_ref):
    tm = x_ref.shape[0]
    h = _rms_adaln(x_ref[...], g_ref[...], mod_ref[0:1, :], mod_ref[1:2, :])
    hb = h.astype(BF16)
    cos = cos_ref[...]
    sin = sin_ref[...]
    lane = lax.broadcasted_iota(jnp.int32, (1, LANES), 1)
    low = lane < HEAD_DIM

    def proj(i):
        return jnp.dot(hb, w_ref[:, IN_OFFSETS[i]:IN_OFFSETS[i + 1]], preferred_element_type=F32)

    def tiles(y):
        return [y[:, t * LANES:(t + 1) * LANES] for t in range(y.shape[1] // LANES)]

    def both_halves(t):
        r = pltpu.roll(t, HEAD_DIM, 1)
        return jnp.where(low, t, r), jnp.where(low, r, t)

    for t, y in enumerate(tiles(proj(0))):
        o = _head_norm_rope(y, hn_ref[0:1, :], cos, sin, lane) * Q_SCALE
        qa_ref[:, t * LANES:(t + 1) * LANES] = o.astype(BF16)
    k0, k1 = both_halves(_head_norm_rope(proj(1), hn_ref[1:2, :], cos, sin, lane))
    ka_ref[:, 0:LANES] = k0.astype(BF16)
    ka_ref[:, LANES:2 * LANES] = k1.astype(BF16)
    for t, y in enumerate(tiles(proj(3))):
        o = _head_norm_rope(y, hn_ref[2:3, :], cos, sin, lane) * Q_SCALE
        qb_ref[:, t * LANES:(t + 1) * LANES] = o.astype(BF16)
    for t, y in enumerate(tiles(proj(4))):
        o = _head_norm_rope(y, hn_ref[3:4, :], cos, sin, lane)
        kb_ref[:, t * LANES:(t + 1) * LANES] = o.astype(BF16)
        for j in range(tm // MOBA_BLOCK):
            blk = o[j * MOBA_BLOCK:(j + 1) * MOBA_BLOCK]
            km_ref[0, j:j + 1, t * LANES:(t + 1) * LANES] = jnp.mean(blk, axis=0, keepdims=True)
    tn = (((0,), (1,)), ((), ()))
    vat_ref[0] = lax.dot_general(w_ref[:, IN_OFFSETS[2]:IN_OFFSETS[3]], hb, tn,
                                 preferred_element_type=F32).astype(BF16)
    vbt_ref[0] = lax.dot_general(w_ref[:, IN_OFFSETS[5]:IN_OFFSETS[6]], hb, tn,
                                 preferred_element_type=F32).astype(BF16)
    ga_ref[...] = jax.nn.sigmoid(proj(6)).astype(BF16)
    gb_ref[...] = jax.nn.sigmoid(proj(7)).astype(BF16)


def _inproj_call(x2d, mod, g_mix, w_in, cos, sin, head_gains, layer, seq, tm):
    tokens, d = x2d.shape
    batch = tokens // seq
    steps_per_seq = seq // tm
    nkm = tm // MOBA_BLOCK
    row = lambda i: (i, 0)
    of_layer = lambda i: (layer, 0, 0)
    seq_major = lambda i: (i // steps_per_seq, 0, i % steps_per_seq)

    def out(width, dtype=BF16):
        return jax.ShapeDtypeStruct((tokens, width), dtype)

    return pl.pallas_call(
        _inproj_kernel,
        grid=(tokens // tm,),
        in_specs=[
            pl.BlockSpec((tm, d), row),
            pl.BlockSpec((None, None, 6, d), lambda i: (layer, i // steps_per_seq, 0, 0)),
            pl.BlockSpec((None, 1, d), of_layer),
            pl.BlockSpec((None, d, IN_W), of_layer),
            pl.BlockSpec((tm, LANES), row),
            pl.BlockSpec((tm, LANES), row),
            pl.BlockSpec((None, 4, LANES), of_layer),
        ],
        out_specs=[
            pl.BlockSpec((tm, SWA_Q_W), row),
            pl.BlockSpec((tm, 2 * LANES), row),
            pl.BlockSpec((1, SWA_KV_W, tm), seq_major),
            pl.BlockSpec((tm, MOBA_W), row),
            pl.BlockSpec((tm, MOBA_W), row),
            pl.BlockSpec((1, MOBA_W, tm), seq_major),
            pl.BlockSpec((tm, d), row),
            pl.BlockSpec((tm, d), row),
            pl.BlockSpec((1, nkm, MOBA_W), lambda i: (i, 0, 0)),
        ],
        out_shape=[
            out(SWA_Q_W), out(2 * LANES),
            jax.ShapeDtypeStruct((batch, SWA_KV_W, seq), BF16),
            out(MOBA_W), out(MOBA_W),
            jax.ShapeDtypeStruct((batch, MOBA_W, seq), BF16),
            out(d), out(d),
            jax.ShapeDtypeStruct((tokens // tm, nkm, MOBA_W), F32),
        ],
        compiler_params=pltpu.CompilerParams(vmem_limit_bytes=VMEM_LIMIT),
        name="in_proj",
    )(x2d, mod, g_mix, w_in, cos, sin, head_gains)


def _swa_kernel(sink_ref, q_ref, k_ref, vt_ref, o_ref, va_scr, s_scr, p_scr, mask_scr, *, tq):
    W = SWA_WINDOW
    seq = q_ref.shape[1]
    nk = tq + W
    group = SWA_Q_HEADS // SWA_KV_HEADS
    lane = lax.broadcasted_iota(jnp.int32, (1, LANES), 1)
    low = lane < HEAD_DIM
    nt = (((1,), (1,)), ((), ()))
    key = lax.broadcasted_iota(jnp.int32, (nk, tq), 0)
    qry = lax.broadcasted_iota(jnp.int32, (nk, tq), 1)

    for slot, lead in enumerate((0, W)):
        diff = qry - key + lead
        mask_scr[slot] = jnp.where((diff >= 0) & (diff < W), 0.0, NEG_INF)

    ones = jnp.ones((SUM_ROWS, seq), BF16)
    for kvh in range(SWA_KV_HEADS):
        va_scr[kvh] = jnp.concatenate([vt_ref[0, kvh * HEAD_DIM:(kvh + 1) * HEAD_DIM, :], ones], axis=0)

    units = [(qi, h) for qi in range(seq // tq) for h in range(SWA_Q_HEADS)]
    col_max, res = {}, {}

    def first_key(qi):
        return max(qi * tq - W, 0)

    def score_step(u):
        qi, h = units[u]
        t, j = divmod(h, HEADS_PER_TILE)
        kvh = h // group
        k0 = first_key(qi)
        q2 = q_ref[0, qi * tq:(qi + 1) * tq, t * LANES:(t + 1) * LANES]
        qh = jnp.where(low if j == 0 else ~low, q2, jnp.zeros_like(q2))
        s = lax.dot_general(k_ref[0, k0:k0 + nk, kvh * LANES:(kvh + 1) * LANES], qh, nt,
                            preferred_element_type=F32)
        s = s + mask_scr[min(qi, 1)]
        s_scr[u % 2] = s
        col_max[u] = jnp.maximum(jnp.max(s, axis=0, keepdims=True), sink_ref[h] * LOG2E)

    def exp_step(u):
        p_scr[u % 2] = jnp.exp2(s_scr[u % 2] - col_max[u]).astype(BF16)

    def value_step(u):
        qi, h = units[u]
        t, j = divmod(h, HEADS_PER_TILE)
        k0 = first_key(qi)
        o = jnp.dot(va_scr[h // group, :, k0:k0 + nk], p_scr[u % 2], preferred_element_type=F32)
        denom = o[HEAD_DIM:HEAD_DIM + 1, :] + jnp.exp2(sink_ref[h] * LOG2E - col_max.pop(u))
        res[j] = o[0:HEAD_DIM] / denom
        if j == HEADS_PER_TILE - 1:
            o_ref[0, qi * tq:(qi + 1) * tq, t * LANES:(t + 1) * LANES] = (
                jnp.concatenate([res[0], res[1]], axis=0).T.astype(BF16))

    for r in range(len(units) + 2):
        for fn, u in ((score_step, r), (exp_step, r - 1), (value_step, r - 2)):
            if 0 <= u < len(units):
                fn(u)


def _swa_call(sinks, qa, ka, vat, tq):
    batch, seq, _ = qa.shape
    assert seq % tq == 0 and seq >= tq + SWA_WINDOW and tq % SWA_WINDOW == 0
    return pl.pallas_call(
        functools.partial(_swa_kernel, tq=tq),
        grid=(batch,),
        in_specs=[
            pl.BlockSpec(memory_space=pltpu.SMEM),
            pl.BlockSpec((1, seq, SWA_Q_W), lambda b: (b, 0, 0)),
            pl.BlockSpec((1, seq, 2 * LANES), lambda b: (b, 0, 0)),
            pl.BlockSpec((1, SWA_KV_W, seq), lambda b: (b, 0, 0)),
        ],
        out_specs=pl.BlockSpec((1, seq, SWA_Q_W), lambda b: (b, 0, 0)),
        out_shape=jax.ShapeDtypeStruct((batch, seq, SWA_Q_W), BF16),
        scratch_shapes=[
            pltpu.VMEM((SWA_KV_HEADS, PV_ROWS, seq), BF16),
            pltpu.VMEM((2, tq + SWA_WINDOW, tq), F32),
            pltpu.VMEM((2, tq + SWA_WINDOW, tq), BF16),
            pltpu.VMEM((2, tq + SWA_WINDOW, tq), F32),
        ],
        compiler_params=pltpu.CompilerParams(vmem_limit_bytes=VMEM_LIMIT),
        name="swa_attn",
    )(sinks, qa, ka, vat)


def _moba_bias_t(gate_t, nblk):
    n = lax.broadcasted_iota(jnp.int32, gate_t.shape, 0)
    qblk = lax.broadcasted_iota(jnp.int32, gate_t.shape, 1) // MOBA_BLOCK
    in_range = n < nblk
    past = n < qblk
    nf = n.astype(F32)
    g = jnp.where(past, gate_t, NEG_INF)
    sel = jnp.zeros(gate_t.shape, jnp.bool_)
    for _ in range(MOBA_TOPK):
        top = jnp.max(g, axis=0, keepdims=True)
        idx = jnp.min(jnp.where(g == top, nf, float(LANES)), axis=0, keepdims=True)
        hit = nf == idx
        sel = sel | hit
        g = jnp.where(hit, -jnp.inf, g)
    allowed = (sel & past) | (n == qblk)
    return jnp.where(in_range & ~allowed, NEG_INF, 0.0)


def _moba_kernel(q_ref, k_ref, vt_ref, km_ref, e_ref, o_ref, qa_scr, ka_scr, va_scr, s_scr, p_scr):
    L = MOBA_BLOCK
    seq = q_ref.shape[1]
    nblk = seq // L
    gate_rows = 16
    lane = lax.broadcasted_iota(jnp.int32, (1, LANES), 1)
    low = lane < HEAD_DIM
    nt = (((1,), (1,)), ((), ()))
    causal_t = (lax.broadcasted_iota(jnp.int32, (L, L), 0) <= lax.broadcasted_iota(jnp.int32, (L, L), 1))

    q_all = q_ref[0]
    k2 = k_ref[0]
    vt = vt_ref[0]
    e = e_ref[...]
    km = jnp.concatenate([km_ref[0], jnp.zeros((gate_rows - nblk, LANES), F32)], axis=0)
    km_hi, km_lo = _split_bf16(km)
    zeros = lambda rows: jnp.zeros((rows, seq), F32)

    ones = jnp.ones((SUM_ROWS, seq), BF16)
    for j in range(HEADS_PER_TILE):
        keep = low if j == 0 else ~low
        off = HEAD_DIM if j == 0 else 0
        qh = jnp.where(keep, q_all, jnp.zeros_like(q_all))
        gate_t = (lax.dot_general(km_hi, qh, nt, preferred_element_type=F32)
                  + lax.dot_general(km_lo, qh, nt, preferred_element_type=F32))
        bias_t = _moba_bias_t(gate_t, nblk)
        parts = ([zeros(off)] if off else []) + [bias_t, zeros(LANES - off - gate_rows)]
        bias = jnp.concatenate(parts, axis=0).T
        qa_scr[j] = jnp.where(keep, q_all, bias.astype(BF16))
        ka_scr[j] = jnp.where(keep, k2, e)
        va_scr[j] = jnp.concatenate([vt[j * HEAD_DIM:(j + 1) * HEAD_DIM], ones], axis=0)

    units = [(qi, j) for qi in range(nblk) for j in range(HEADS_PER_TILE)]
    col_max, acc, res = {}, {}, {}

    def score_step(u, c):
        qi, j = units[u]
        s = lax.dot_general(ka_scr[j, c * L:(c + 1) * L, :], qa_scr[j, qi * L:(qi + 1) * L, :], nt,
                            preferred_element_type=F32)
        if c == qi:
            s = jnp.where(causal_t, s, NEG_INF)
        s_scr[u % 2, c * L:(c + 1) * L, :] = s
        m = jnp.max(s, axis=0, keepdims=True)
        col_max[u] = m if c == 0 else jnp.maximum(col_max[u], m)

    def exp_step(u, c):
        p_scr[u % 2, c * L:(c + 1) * L, :] = jnp.exp2(
            s_scr[u % 2, c * L:(c + 1) * L, :] - col_max[u]).astype(BF16)

    def value_step(u, c):
        qi, j = units[u]
        part = jnp.dot(va_scr[j, :, c * L:(c + 1) * L], p_scr[u % 2, c * L:(c + 1) * L, :],
                       preferred_element_type=F32)
        acc[u] = part if c == 0 else acc[u] + part
        if c == qi:
            o = acc.pop(u)
            res[j] = o[0:HEAD_DIM] / o[HEAD_DIM:HEAD_DIM + 1, :]
            if j == HEADS_PER_TILE - 1:
                o_ref[0, qi * L:(qi + 1) * L, :] = jnp.concatenate([res[0], res[1]], axis=0).T.astype(BF16)

    def steps(fn, u):
        if not 0 <= u < len(units):
            return []
        return [functools.partial(fn, u, c) for c in range(units[u][0] + 1)]

    for r in range(len(units) + 2):
        stages = [steps(score_step, r), steps(exp_step, r - 1), steps(value_step, r - 2)]
        for i in range(max(len(s) for s in stages)):
            for stage in stages:
                if i < len(stage):
                    stage[i]()


def _moba_call(qb, kb, vbt, km):
    batch, seq, width = qb.shape
    nblk = seq // MOBA_BLOCK
    blk_of_key = np.arange(seq)[:, None] // MOBA_BLOCK
    lane = np.arange(LANES)[None, :]
    onehot = jnp.asarray((lane % HEAD_DIM == blk_of_key), dtype=BF16)
    seq_spec = pl.BlockSpec((1, seq, LANES), lambda b, h: (b, 0, h))
    return pl.pallas_call(
        _moba_kernel,
        grid=(batch, width // LANES),
        in_specs=[
            seq_spec, seq_spec,
            pl.BlockSpec((1, LANES, seq), lambda b, h: (b, h, 0)),
            pl.BlockSpec((1, nblk, LANES), lambda b, h: (b, 0, h)),
            pl.BlockSpec((seq, LANES), lambda b, h: (0, 0)),
        ],
        out_specs=seq_spec,
        out_shape=jax.ShapeDtypeStruct((batch, seq, width), BF16),
        scratch_shapes=[
            pltpu.VMEM((HEADS_PER_TILE, seq, LANES), BF16),
            pltpu.VMEM((HEADS_PER_TILE, seq, LANES), BF16),
            pltpu.VMEM((HEADS_PER_TILE, PV_ROWS, seq), BF16),
            pltpu.VMEM((2, seq, MOBA_BLOCK), F32),
            pltpu.VMEM((2, seq, MOBA_BLOCK), BF16),
        ],
        compiler_params=pltpu.CompilerParams(vmem_limit_bytes=VMEM_LIMIT),
        name="moba_attn",
    )(qb, kb, vbt, km, onehot)


def _post_kernel(x_ref, oa_ref, ob_ref, ga_ref, gb_ref, mod_ref, g_ref,
                 woa_ref, wob_ref, wout_ref, wup_ref, wdn_ref, o_ref):
    ya = jnp.dot(oa_ref[...], woa_ref[...], preferred_element_type=F32)
    yb = jnp.dot(ob_ref[...], wob_ref[...], preferred_element_type=F32)
    mixed = ga_ref[...].astype(F32) * ya + gb_ref[...].astype(F32) * yb
    y = jnp.dot(mixed.astype(BF16), wout_ref[...], preferred_element_type=F32)
    x1 = x_ref[...] + mod_ref[2:3, :] * y
    h = _rms_adaln(x1, g_ref[...], mod_ref[3:4, :], mod_ref[4:5, :]).astype(BF16)
    acc = jnp.zeros(x1.shape, F32)
    for c in range(D_FF // FF_CHUNK):
        u = jnp.dot(h, wup_ref[:, c * FF_CHUNK:(c + 1) * FF_CHUNK], preferred_element_type=F32)
        u = jnp.square(jnp.maximum(u, 0.0)).astype(BF16)
        acc += jnp.dot(u, wdn_ref[c * FF_CHUNK:(c + 1) * FF_CHUNK, :], preferred_element_type=F32)
    o_ref[...] = x1 + mod_ref[5:6, :] * acc


def _post_call(x2d, oa, ob, ga, gb, mod, g_mlp, woa, wob, wout, wup, wdn, layer, seq, tm):
    tokens, d = x2d.shape
    steps_per_seq = seq // tm
    row = lambda i: (i, 0)
    of_layer = lambda i: (layer, 0, 0)

    def resident(w):
        return pl.BlockSpec((None,) + w.shape[1:], of_layer, pipeline_mode=pl.Buffered(1))

    return pl.pallas_call(
        _post_kernel,
        grid=(tokens // tm,),
        in_specs=[
            pl.BlockSpec((tm, d), row),
            pl.BlockSpec((tm, SWA_Q_W), row),
            pl.BlockSpec((tm, MOBA_W), row),
            pl.BlockSpec((tm, d), row),
            pl.BlockSpec((tm, d), row),
            pl.BlockSpec((None, None, 6, d), lambda i: (layer, i // steps_per_seq, 0, 0)),
            pl.BlockSpec((None, 1, d), of_layer),
            resident(woa), resident(wob), resident(wout), resident(wup), resident(wdn),
        ],
        out_specs=pl.BlockSpec((tm, d), row),
        out_shape=jax.ShapeDtypeStruct((tokens, d), F32),
        compiler_params=pltpu.CompilerParams(vmem_limit_bytes=VMEM_LIMIT),
        name="post_mlp",
    )(x2d, oa, ob, ga, gb, mod, g_mlp, woa, wob, wout, wup, wdn)


def kernel(x, c, positions, rms_mix, rms_mlp, w_ada, b_ada, w_in, q_norm_swa, k_norm_swa, q_norm_moba,
           k_norm_moba, swa_sinks, w_o_swa, w_o_moba, w_out, w_up, w_down):
    batch, seq, d = x.shape
    depth = w_in.shape[0]
    tokens = batch * seq
    assert seq % MOBA_BLOCK == 0 and d == D_MODEL

    mod = _ada_call(c, w_ada, b_ada).reshape(depth, batch, 6, d)
    cos, sin = _rope_tables(positions)
    w_in, w_o_swa, w_o_moba, w_out, w_up, w_down = (
        w.astype(BF16) for w in (w_in, w_o_swa, w_o_moba, w_out, w_up, w_down))
    gains = jnp.stack([jnp.tile(g, (1, HEADS_PER_TILE))
                       for g in (q_norm_swa, k_norm_swa, q_norm_moba, k_norm_moba)], axis=1)
    rms_mix = rms_mix.reshape(depth, 1, d)
    rms_mlp = rms_mlp.reshape(depth, 1, d)

    x2d = x.reshape(tokens, d)
    for l in range(depth):
        qa, ka, vat, qb, kb, vbt, ga, gb, km = _inproj_call(
            x2d, mod, rms_mix, w_in, cos, sin, gains, l, seq, tm=ROW_TILE)
        nblk = seq // MOBA_BLOCK
        oa = _swa_call(swa_sinks[l], qa.reshape(batch, seq, -1), ka.reshape(batch, seq, -1), vat, tq=SWA_Q_TILE)
        ob = _moba_call(qb.reshape(batch, seq, -1), kb.reshape(batch, seq, -1), vbt,
                        km.reshape(batch, nblk, MOBA_W))
        x2d = _post_call(x2d, oa.reshape(tokens, -1), ob.reshape(tokens, -1), ga, gb, mod, rms_mlp,
                         w_o_swa, w_o_moba, w_out, w_up, w_down, l, seq, tm=ROW_TILE)
    return x2d.reshape(batch, seq, d)
```

```python
import functools

import jax
import jax.numpy as jnp
import numpy as np
from jax import lax
from jax.experimental import pallas as pl
from jax.experimental.pallas import tpu as pltpu

D_MODEL = 1024
HEAD_DIM = 64
SWA_Q_HEADS = 8
SWA_KV_HEADS = 2
SWA_WINDOW = 128
MOBA_HEADS = 8
MOBA_BLOCK = 256
MOBA_TOPK = 3
D_FF = 4 * D_MODEL
ROPE_THETA = 10000.0
NORM_EPS = 1e-6
NEG_INF = -1e30

SWA_Q_W = SWA_Q_HEADS * HEAD_DIM
SWA_KV_W = SWA_KV_HEADS * HEAD_DIM
MOBA_W = MOBA_HEADS * HEAD_DIM
IN_SPLITS = (SWA_Q_W, SWA_KV_W, SWA_KV_W, MOBA_W, MOBA_W, MOBA_W, D_MODEL, D_MODEL)
IN_W = sum(IN_SPLITS)
IN_OFFSETS = tuple(int(o) for o in np.cumsum((0,) + IN_SPLITS))

LANES = 128
HEADS_PER_TILE = LANES // HEAD_DIM
LOG2E = 1.4426950408889634
Q_SCALE = HEAD_DIM ** -0.5 * LOG2E
SUM_ROWS = 16
PV_ROWS = HEAD_DIM + SUM_ROWS
VMEM_LIMIT = 56 * 1024 * 1024
ROW_TILE = 512
SWA_Q_TILE = 256
FF_CHUNK = 1024

BF16 = jnp.bfloat16
F32 = jnp.float32


def _split_bf16(a):
    hi = a.astype(BF16)
    lo = (a - hi.astype(F32)).astype(BF16)
    return hi, lo


def _ada_kernel(c_ref, w_ref, b_ref, o_ref):
    c = c_ref[...]
    a = c * jax.nn.sigmoid(c)
    a_hi, a_lo = _split_bf16(a)
    w_hi, w_lo = _split_bf16(w_ref[0])
    acc = jnp.dot(a_hi, w_hi, preferred_element_type=F32)
    acc += jnp.dot(a_hi, w_lo, preferred_element_type=F32)
    acc += jnp.dot(a_lo, w_hi, preferred_element_type=F32)
    o_ref[0] = acc + b_ref[0]


def _ada_call(c, w_ada, b_ada):
    depth, d, n = w_ada.shape
    batch = c.shape[0]
    tn = 1024
    return pl.pallas_call(
        _ada_kernel,
        grid=(depth, n // tn),
        in_specs=[
            pl.BlockSpec((batch, d), lambda l, j: (0, 0)),
            pl.BlockSpec((1, d, tn), lambda l, j: (l, 0, j)),
            pl.BlockSpec((1, 1, tn), lambda l, j: (l, 0, j)),
        ],
        out_specs=pl.BlockSpec((1, batch, tn), lambda l, j: (l, 0, j)),
        out_shape=jax.ShapeDtypeStruct((depth, batch, n), F32),
        compiler_params=pltpu.CompilerParams(vmem_limit_bytes=VMEM_LIMIT),
        name="ada_mod",
    )(c, w_ada, b_ada.reshape(depth, 1, n))


def _rope_table_kernel(pos_ref, inv_ref, cos_ref, sin_ref):
    ang = pos_ref[0] * inv_ref[...]
    cos_ref[0] = jnp.cos(ang)
    sin_ref[0] = jnp.sin(ang)


def _rope_tables(positions):
    batch, seq = positions.shape
    half = HEAD_DIM // 2
    per_row = LANES // half
    inv_freq = ROPE_THETA ** (-jnp.arange(half, dtype=F32) / half)
    inv_tile = jnp.tile(inv_freq, per_row).reshape(1, LANES)
    pos = jnp.repeat(positions.astype(F32).reshape(batch, seq // per_row, per_row), half, axis=2)
    spec = pl.BlockSpec((1, seq // per_row, LANES), lambda b: (b, 0, 0))
    out = jax.ShapeDtypeStruct((batch, seq // per_row, LANES), F32)
    cos, sin = pl.pallas_call(
        _rope_table_kernel,
        grid=(batch,),
        in_specs=[spec, pl.BlockSpec((1, LANES), lambda b: (0, 0))],
        out_specs=[spec, spec],
        out_shape=[out, out],
        name="rope_tables",
    )(pos, inv_tile)
    cos = cos.reshape(batch * seq, half)
    sin = sin.reshape(batch * seq, half)
    return jnp.tile(cos, (1, per_row)), jnp.concatenate([-sin, sin] * (per_row // 2), axis=1)


def _rms_adaln(x, gain, shift, scale):
    ms = jnp.mean(x * x, axis=-1, keepdims=True)
    return (x * lax.rsqrt(ms + NORM_EPS) * gain) * (1.0 + scale) + shift


def _head_norm_rope(y, gain, cos, sin, lane):
    low = lane < HEAD_DIM
    y2 = y * y
    s_low = jnp.sum(jnp.where(low, y2, 0.0), axis=-1, keepdims=True)
    s_high = jnp.sum(jnp.where(low, 0.0, y2), axis=-1, keepdims=True)
    ms = jnp.where(low, s_low, s_high) * (1.0 / HEAD_DIM)
    yn = y * lax.rsqrt(ms + NORM_EPS) * gain
    first_half = (lane % HEAD_DIM) < (HEAD_DIM // 2)
    partner = jnp.where(first_half,
                        pltpu.roll(yn, LANES - HEAD_DIM // 2, 1),
                        pltpu.roll(yn, HEAD_DIM // 2, 1))
    return yn * cos + partner * sin


def _inproj_kernel(x_ref, mod_ref, g_ref, w_ref, cos_ref, sin_ref, hn_ref,
                   qa_ref, ka_ref, vat_ref, qb_ref, kb_ref, vbt_ref, ga_ref, gb_ref):
    h = _rms_adaln(x_ref[...], g_ref[...], mod_ref[0:1, :], mod_ref[1:2, :])
    hb = h.astype(BF16)
    cos = cos_ref[...]
    sin = sin_ref[...]
    lane = lax.broadcasted_iota(jnp.int32, (1, LANES), 1)
    low = lane < HEAD_DIM

    def proj(i):
        return jnp.dot(hb, w_ref[:, IN_OFFSETS[i]:IN_OFFSETS[i + 1]], preferred_element_type=F32)

    def tiles(y):
        return [y[:, t * LANES:(t + 1) * LANES] for t in range(y.shape[1] // LANES)]

    def both_halves(t):
        r = pltpu.roll(t, HEAD_DIM, 1)
        return jnp.where(low, t, r), jnp.where(low, r, t)

    for t, y in enumerate(tiles(proj(0))):
        o = _head_norm_rope(y, hn_ref[0:1, :], cos, sin, lane) * Q_SCALE
        qa_ref[:, t * LANES:(t + 1) * LANES] = o.astype(BF16)
    k0, k1 = both_halves(_head_norm_rope(proj(1), hn_ref[1:2, :], cos, sin, lane))
    ka_ref[:, 0:LANES] = k0.astype(BF16)
    ka_ref[:, LANES:2 * LANES] = k1.astype(BF16)
    for t, y in enumerate(tiles(proj(3))):
        o = _head_norm_rope(y, hn_ref[2:3, :], cos, sin, lane) * Q_SCALE
        qb_ref[:, t * LANES:(t + 1) * LANES] = o.astype(BF16)
    for t, y in enumerate(tiles(proj(4))):
        o = _head_norm_rope(y, hn_ref[3:4, :], cos, sin, lane)
        kb_ref[:, t * LANES:(t + 1) * LANES] = o.astype(BF16)
    tn = (((0,), (1,)), ((), ()))
    vat_ref[0] = lax.dot_general(w_ref[:, IN_OFFSETS[2]:IN_OFFSETS[3]], hb, tn,
                                 preferred_element_type=F32).astype(BF16)
    vbt_ref[0] = lax.dot_general(w_ref[:, IN_OFFSETS[5]:IN_OFFSETS[6]], hb, tn,
                                 preferred_element_type=F32).astype(BF16)
    ga_ref[...] = jax.nn.sigmoid(proj(6)).astype(BF16)
    gb_ref[...] = jax.nn.sigmoid(proj(7)).astype(BF16)


def _inproj_call(x2d, mod, g_mix, w_in, cos, sin, head_gains, layer, seq, tm):
    tokens, d = x2d.shape
    batch = tokens // seq
    steps_per_seq = seq // tm
    row = lambda i: (i, 0)
    of_layer = lambda i: (layer, 0, 0)
    seq_major = lambda i: (i // steps_per_seq, 0, i % steps_per_seq)

    def out(width, dtype=BF16):
        return jax.ShapeDtypeStruct((tokens, width), dtype)

    return pl.pallas_call(
        _inproj_kernel,
        grid=(tokens // tm,),
        in_specs=[
            pl.BlockSpec((tm, d), row),
            pl.BlockSpec((None, None, 6, d), lambda i: (layer, i // steps_per_seq, 0, 0)),
            pl.BlockSpec((None, 1, d), of_layer),
            pl.BlockSpec((None, d, IN_W), of_layer),
            pl.BlockSpec((tm, LANES), row),
            pl.BlockSpec((tm, LANES), row),
            pl.BlockSpec((None, 4, LANES), of_layer),
        ],
        out_specs=[
            pl.BlockSpec((tm, SWA_Q_W), row),
            pl.BlockSpec((tm, 2 * LANES), row),
            pl.BlockSpec((1, SWA_KV_W, tm), seq_major),
            pl.BlockSpec((tm, MOBA_W), row),
            pl.BlockSpec((tm, MOBA_W), row),
            pl.BlockSpec((1, MOBA_W, tm), seq_major),
            pl.BlockSpec((tm, d), row),
            pl.BlockSpec((tm, d), row),
        ],
        out_shape=[
            out(SWA_Q_W), out(2 * LANES),
            jax.ShapeDtypeStruct((batch, SWA_KV_W, seq), BF16),
            out(MOBA_W), out(MOBA_W),
            jax.ShapeDtypeStruct((batch, MOBA_W, seq), BF16),
            out(d), out(d),
        ],
        compiler_params=pltpu.CompilerParams(vmem_limit_bytes=VMEM_LIMIT),
        name="in_proj",
    )(x2d, mod, g_mix, w_in, cos, sin, head_gains)


def _swa_kernel(sink_ref, q_ref, k_ref, vt_ref, o_ref, va_scr, s_scr, p_scr, mask_scr, *, tq):
    W = SWA_WINDOW
    seq = q_ref.shape[1]
    nk = tq + W
    group = SWA_Q_HEADS // SWA_KV_HEADS
    lane = lax.broadcasted_iota(jnp.int32, (1, LANES), 1)
    low = lane < HEAD_DIM
    nt = (((1,), (1,)), ((), ()))
    key = lax.broadcasted_iota(jnp.int32, (nk, tq), 0)
    qry = lax.broadcasted_iota(jnp.int32, (nk, tq), 1)

    for slot, lead in enumerate((0, W)):
        diff = qry - key + lead
        mask_scr[slot] = jnp.where((diff >= 0) & (diff < W), 0.0, NEG_INF)

    ones = jnp.ones((SUM_ROWS, seq), BF16)
    for kvh in range(SWA_KV_HEADS):
        va_scr[kvh] = jnp.concatenate([vt_ref[0, kvh * HEAD_DIM:(kvh + 1) * HEAD_DIM, :], ones], axis=0)

    units = [(qi, h) for qi in range(seq // tq) for h in range(SWA_Q_HEADS)]
    col_max, res = {}, {}

    def first_key(qi):
        return max(qi * tq - W, 0)

    def score_step(u):
        qi, h = units[u]
        t, j = divmod(h, HEADS_PER_TILE)
        kvh = h // group
        k0 = first_key(qi)
        q2 = q_ref[0, qi * tq:(qi + 1) * tq, t * LANES:(t + 1) * LANES]
        qh = jnp.where(low if j == 0 else ~low, q2, jnp.zeros_like(q2))
        s = lax.dot_general(k_ref[0, k0:k0 + nk, kvh * LANES:(kvh + 1) * LANES], qh, nt,
                            preferred_element_type=F32)
        s = s + mask_scr[min(qi, 1)]
        s_scr[u % 2] = s
        col_max[u] = jnp.maximum(jnp.max(s, axis=0, keepdims=True), sink_ref[h] * LOG2E)

    def exp_step(u):
        p_scr[u % 2] = jnp.exp2(s_scr[u % 2] - col_max[u]).astype(BF16)

    def value_step(u):
        qi, h = units[u]
        t, j = divmod(h, HEADS_PER_TILE)
        k0 = first_key(qi)
        o = jnp.dot(va_scr[h // group, :, k0:k0 + nk], p_scr[u % 2], preferred_element_type=F32)
        denom = o[HEAD_DIM:HEAD_DIM + 1, :] + jnp.exp2(sink_ref[h] * LOG2E - col_max.pop(u))
        res[j] = o[0:HEAD_DIM] / denom
        if j == HEADS_PER_TILE - 1:
            o_ref[0, qi * tq:(qi + 1) * tq, t * LANES:(t + 1) * LANES] = (
                jnp.concatenate([res[0], res[1]], axis=0).T.astype(BF16))

    for r in range(len(units) + 2):
        for fn, u in ((score_step, r), (exp_step, r - 1), (value_step, r - 2)):
            if 0 <= u < len(units):
                fn(u)


def _swa_call(sinks, qa, ka, vat, tq):
    batch, seq, _ = qa.shape
    assert seq % tq == 0 and seq >= tq + SWA_WINDOW and tq % SWA_WINDOW == 0
    return pl.pallas_call(
        functools.partial(_swa_kernel, tq=tq),
        grid=(batch,),
        in_specs=[
            pl.BlockSpec(memory_space=pltpu.SMEM),
            pl.BlockSpec((1, seq, SWA_Q_W), lambda b: (b, 0, 0)),
            pl.BlockSpec((1, seq, 2 * LANES), lambda b: (b, 0, 0)),
            pl.BlockSpec((1, SWA_KV_W, seq), lambda b: (b, 0, 0)),
        ],
        out_specs=pl.BlockSpec((1, seq, SWA_Q_W), lambda b: (b, 0, 0)),
        out_shape=jax.ShapeDtypeStruct((batch, seq, SWA_Q_W), BF16),
        scratch_shapes=[
            pltpu.VMEM((SWA_KV_HEADS, PV_ROWS, seq), BF16),
            pltpu.VMEM((2, tq + SWA_WINDOW, tq), F32),
            pltpu.VMEM((2, tq + SWA_WINDOW, tq), BF16),
            pltpu.VMEM((2, tq + SWA_WINDOW, tq), F32),
        ],
        compiler_params=pltpu.CompilerParams(vmem_limit_bytes=VMEM_LIMIT),
        name="swa_attn",
    )(sinks, qa, ka, vat)


def _moba_offsets(col_max, col_sum, nblk):
    n_past = len(col_sum)
    keep = []
    for n in range(n_past):
        rank = jnp.where(col_sum[n] < NEG_INF, float(nblk - n_past), 0.0)
        for m in range(n_past):
            if m != n:
                before = (col_sum[m] >= col_sum[n]) if m < n else (col_sum[m] > col_sum[n])
                rank = rank + jnp.where(before, 1.0, 0.0)
        keep.append(rank < MOBA_TOPK)
    top = col_max[n_past]
    for n in range(n_past):
        top = jnp.maximum(top, jnp.where(keep[n], col_max[n], NEG_INF))
    return [jnp.where(keep[n], top, -NEG_INF) for n in range(n_past)] + [top]


def _moba_kernel(q_ref, k_ref, vt_ref, o_ref, qh_scr, va_scr, s0_scr, s1_scr, p0_scr, p1_scr):
    L = MOBA_BLOCK
    seq = q_ref.shape[1]
    nblk = seq // L
    lane = lax.broadcasted_iota(jnp.int32, (1, LANES), 1)
    low = lane < HEAD_DIM
    nt = (((1,), (1,)), ((), ()))
    causal_t = (lax.broadcasted_iota(jnp.int32, (L, L), 0) <= lax.broadcasted_iota(jnp.int32, (L, L), 1))

    q_all = q_ref[0]
    vt = vt_ref[0]
    ones = jnp.ones((SUM_ROWS, seq), BF16)
    for j in range(HEADS_PER_TILE):
        qh_scr[j] = jnp.where(low if j == 0 else ~low, q_all, jnp.zeros_like(q_all))
        va_scr[j] = jnp.concatenate([vt[j * HEAD_DIM:(j + 1) * HEAD_DIM], ones], axis=0)

    units = [(qi, j) for qi in range(nblk) for j in range(HEADS_PER_TILE)]
    s_bufs, p_bufs = (s0_scr, s1_scr), (p0_scr, p1_scr)
    col_max, col_sum, offsets, acc, res = {}, {}, {}, {}, {}

    def score_step(u, c):
        qi, j = units[u]
        if c > 0:
            return
        s_all = lax.dot_general(k_ref[0, 0:(qi + 1) * L, :], qh_scr[j, qi * L:(qi + 1) * L, :], nt,
                                preferred_element_type=F32)
        for c in range(qi + 1):
            s = s_all[c * L:(c + 1) * L]
            if c == qi:
                s = jnp.where(causal_t, s, NEG_INF)
            else:
                col_sum.setdefault(u, []).append(jnp.sum(s, axis=0, keepdims=True))
            s_bufs[u % 2][c * L:(c + 1) * L, :] = s
            col_max.setdefault(u, []).append(jnp.max(s, axis=0, keepdims=True))
        offsets[u] = _moba_offsets(col_max.pop(u), col_sum.pop(u, []), nblk)

    def exp_step(u, c):
        p_bufs[u % 2][c * L:(c + 1) * L, :] = jnp.exp2(
            s_bufs[u % 2][c * L:(c + 1) * L, :] - offsets[u][c]).astype(BF16)

    def value_step(u, c):
        qi, j = units[u]
        part = jnp.dot(va_scr[j, :, c * L:(c + 1) * L], p_bufs[u % 2][c * L:(c + 1) * L, :],
                       preferred_element_type=F32)
        acc[u] = part if c == 0 else acc[u] + part
        if c == qi:
            o = acc.pop(u)
            res[j] = o[0:HEAD_DIM] / o[HEAD_DIM:HEAD_DIM + 1, :]
            if j == HEADS_PER_TILE - 1:
                o_ref[0, qi * L:(qi + 1) * L, :] = jnp.concatenate([res[0], res[1]], axis=0).T.astype(BF16)

    def steps(fn, u):
        if not 0 <= u < len(units):
            return []
        return [functools.partial(fn, u, c) for c in range(units[u][0] + 1)]

    for r in range(len(units) + 2):
        stages = [steps(score_step, r), steps(exp_step, r - 1), steps(value_step, r - 2)]
        for i in range(max(len(s) for s in stages)):
            for stage in stages:
                if i < len(stage):
                    stage[i]()


def _moba_call(qb, kb, vbt):
    batch, seq, width = qb.shape
    assert seq % MOBA_BLOCK == 0
    seq_spec = pl.BlockSpec((1, seq, LANES), lambda b, h: (b, 0, h))
    return pl.pallas_call(
        _moba_kernel,
        grid=(batch, width // LANES),
        in_specs=[seq_spec, seq_spec, pl.BlockSpec((1, LANES, seq), lambda b, h: (b, h, 0))],
        out_specs=seq_spec,
        out_shape=jax.ShapeDtypeStruct((batch, seq, width), BF16),
        scratch_shapes=[
            pltpu.VMEM((HEADS_PER_TILE, seq, LANES), BF16),
            pltpu.VMEM((HEADS_PER_TILE, PV_ROWS, seq), BF16),
            pltpu.VMEM((seq, MOBA_BLOCK), F32),
            pltpu.VMEM((seq, MOBA_BLOCK), F32),
            pltpu.VMEM((seq, MOBA_BLOCK), BF16),
            pltpu.VMEM((seq, MOBA_BLOCK), BF16),
        ],
        compiler_params=pltpu.CompilerParams(vmem_limit_bytes=VMEM_LIMIT),
        name="moba_attn",
    )(qb, kb, vbt)


def _post_kernel(x_ref, oa_ref, ob_ref, ga_ref, gb_ref, mod_ref, g_ref,
                 woa_ref, wob_ref, wout_ref, wup_ref, wdn_ref, o_ref):
    ya = jnp.dot(oa_ref[...], woa_ref[...], preferred_element_type=F32)
    yb = jnp.dot(ob_ref[...], wob_ref[...], preferred_element_type=F32)
    mixed = ga_ref[...].astype(F32) * ya + gb_ref[...].astype(F32) * yb
    y = jnp.dot(mixed.astype(BF16), wout_ref[...], preferred_element_type=F32)
    x1 = x_ref[...] + mod_ref[2:3, :] * y
    h = _rms_adaln(x1, g_ref[...], mod_ref[3:4, :], mod_ref[4:5, :]).astype(BF16)
    acc = jnp.zeros(x1.shape, F32)
    for c in range(D_FF // FF_CHUNK):
        u = jnp.dot(h, wup_ref[:, c * FF_CHUNK:(c + 1) * FF_CHUNK], preferred_element_type=F32)
        u = jnp.square(jnp.maximum(u, 0.0)).astype(BF16)
        acc += jnp.dot(u, wdn_ref[c * FF_CHUNK:(c + 1) * FF_CHUNK, :], preferred_element_type=F32)
    o_ref[...] = x1 + mod_ref[5:6, :] * acc


def _post_call(x2d, oa, ob, ga, gb, mod, g_mlp, woa, wob, wout, wup, wdn, layer, seq, tm):
    tokens, d = x2d.shape
    steps_per_seq = seq // tm
    row = lambda i: (i, 0)
    of_layer = lambda i: (layer, 0, 0)

    def resident(w):
        return pl.BlockSpec((None,) + w.shape[1:], of_layer, pipeline_mode=pl.Buffered(1))

    return pl.pallas_call(
        _post_kernel,
        grid=(tokens // tm,),
        in_specs=[
            pl.BlockSpec((tm, d), row),
            pl.BlockSpec((tm, SWA_Q_W), row),
            pl.BlockSpec((tm, MOBA_W), row),
            pl.BlockSpec((tm, d), row),
            pl.BlockSpec((tm, d), row),
            pl.BlockSpec((None, None, 6, d), lambda i: (layer, i // steps_per_seq, 0, 0)),
            pl.BlockSpec((None, 1, d), of_layer),
            resident(woa), resident(wob), resident(wout), resident(wup), resident(wdn),
        ],
        out_specs=pl.BlockSpec((tm, d), row),
        out_shape=jax.ShapeDtypeStruct((tokens, d), F32),
        compiler_params=pltpu.CompilerParams(vmem_limit_bytes=VMEM_LIMIT),
        name="post_mlp",
    )(x2d, oa, ob, ga, gb, mod, g_mlp, woa, wob, wout, wup, wdn)


def kernel(x, c, positions, rms_mix, rms_mlp, w_ada, b_ada, w_in, q_norm_swa, k_norm_swa, q_norm_moba,
           k_norm_moba, swa_sinks, w_o_swa, w_o_moba, w_out, w_up, w_down):
    batch, seq, d = x.shape
    depth = w_in.shape[0]
    tokens = batch * seq
    assert seq % MOBA_BLOCK == 0 and d == D_MODEL

    mod = _ada_call(c, w_ada, b_ada).reshape(depth, batch, 6, d)
    cos, sin = _rope_tables(positions)
    w_in, w_o_swa, w_o_moba, w_out, w_up, w_down = (
        w.astype(BF16) for w in (w_in, w_o_swa, w_o_moba, w_out, w_up, w_down))
    gains = jnp.stack([jnp.tile(g, (1, HEADS_PER_TILE))
                       for g in (q_norm_swa, k_norm_swa, q_norm_moba, k_norm_moba)], axis=1)
    rms_mix = rms_mix.reshape(depth, 1, d)
    rms_mlp = rms_mlp.reshape(depth, 1, d)

    x2d = x.reshape(tokens, d)
    for l in range(depth):
        qa, ka, vat, qb, kb, vbt, ga, gb = _inproj_call(
            x2d, mod, rms_mix, w_in, cos, sin, gains, l, seq, tm=ROW_TILE)
        oa = _swa_call(swa_sinks[l], qa.reshape(batch, seq, -1), ka.reshape(batch, seq, -1), vat, tq=SWA_Q_TILE)
        ob = _moba_call(qb.reshape(batch, seq, -1), kb.reshape(batch, seq, -1), vbt)
        x2d = _post_call(x2d, oa.reshape(tokens, -1), ob.reshape(tokens, -1), ga, gb, mod, rms_mlp,
                         w_o_swa, w_o_moba, w_out, w_up, w_down, l, seq, tm=ROW_TILE)
    return x2d.reshape(batch, seq, d)
```

```python
import functools

import jax
import jax.numpy as jnp
import numpy as np
from jax import lax
from jax.experimental import pallas as pl
from jax.experimental.pallas import tpu as pltpu

D_MODEL = 1024
HEAD_DIM = 64
SWA_Q_HEADS = 8
SWA_KV_HEADS = 2
SWA_WINDOW = 128
MOBA_HEADS = 8
MOBA_BLOCK = 256
MOBA_TOPK = 3
D_FF = 4 * D_MODEL
ROPE_THETA = 10000.0
NORM_EPS = 1e-6
NEG_INF = -1e30

SWA_Q_W = SWA_Q_HEADS * HEAD_DIM
SWA_KV_W = SWA_KV_HEADS * HEAD_DIM
MOBA_W = MOBA_HEADS * HEAD_DIM
IN_SPLITS = (SWA_Q_W, SWA_KV_W, SWA_KV_W, MOBA_W, MOBA_W, MOBA_W, D_MODEL, D_MODEL)
IN_W = sum(IN_SPLITS)
IN_OFFSETS = tuple(int(o) for o in np.cumsum((0,) + IN_SPLITS))

LANES = 128
HEADS_PER_TILE = LANES // HEAD_DIM
LOG2E = 1.4426950408889634
Q_SCALE = HEAD_DIM ** -0.5 * LOG2E
SUM_ROWS = 16
PV_ROWS = HEAD_DIM + SUM_ROWS
VMEM_LIMIT = 56 * 1024 * 1024
ROW_TILE = 512
FF_CHUNK = 1024
SWA_LAG = 4
SWA_SLOTS = SWA_LAG + 1
MOBA_LAG = 2
MOBA_SLOTS = MOBA_LAG + 1

BF16 = jnp.bfloat16
F32 = jnp.float32


def _split_bf16(a):
    hi = a.astype(BF16)
    lo = (a - hi.astype(F32)).astype(BF16)
    return hi, lo


def _ada_kernel(c_ref, w_ref, b_ref, o_ref):
    c = c_ref[...]
    a = c * jax.nn.sigmoid(c)
    a_hi, a_lo = _split_bf16(a)
    w_hi, w_lo = _split_bf16(w_ref[0])
    acc = jnp.dot(a_hi, w_hi, preferred_element_type=F32)
    acc += jnp.dot(a_hi, w_lo, preferred_element_type=F32)
    acc += jnp.dot(a_lo, w_hi, preferred_element_type=F32)
    o_ref[0] = acc + b_ref[0]


def _ada_call(c, w_ada, b_ada):
    depth, d, n = w_ada.shape
    batch = c.shape[0]
    tn = 1024
    return pl.pallas_call(
        _ada_kernel,
        grid=(depth, n // tn),
        in_specs=[
            pl.BlockSpec((batch, d), lambda l, j: (0, 0)),
            pl.BlockSpec((1, d, tn), lambda l, j: (l, 0, j)),
            pl.BlockSpec((1, 1, tn), lambda l, j: (l, 0, j)),
        ],
        out_specs=pl.BlockSpec((1, batch, tn), lambda l, j: (l, 0, j)),
        out_shape=jax.ShapeDtypeStruct((depth, batch, n), F32),
        compiler_params=pltpu.CompilerParams(vmem_limit_bytes=VMEM_LIMIT),
        name="ada_mod",
    )(c, w_ada, b_ada.reshape(depth, 1, n))


def _rope_table_kernel(pos_ref, inv_ref, cos_ref, sin_ref):
    ang = pos_ref[0] * inv_ref[...]
    cos_ref[0] = jnp.cos(ang)
    sin_ref[0] = jnp.sin(ang)


def _rope_tables(positions):
    batch, seq = positions.shape
    half = HEAD_DIM // 2
    per_row = LANES // half
    inv_freq = ROPE_THETA ** (-jnp.arange(half, dtype=F32) / half)
    inv_tile = jnp.tile(inv_freq, per_row).reshape(1, LANES)
    pos = jnp.repeat(positions.astype(F32).reshape(batch, seq // per_row, per_row), half, axis=2)
    spec = pl.BlockSpec((1, seq // per_row, LANES), lambda b: (b, 0, 0))
    out = jax.ShapeDtypeStruct((batch, seq // per_row, LANES), F32)
    cos, sin = pl.pallas_call(
        _rope_table_kernel,
        grid=(batch,),
        in_specs=[spec, pl.BlockSpec((1, LANES), lambda b: (0, 0))],
        out_specs=[spec, spec],
        out_shape=[out, out],
        name="rope_tables",
    )(pos, inv_tile)
    cos = cos.reshape(batch * seq, half)
    sin = sin.reshape(batch * seq, half)
    return jnp.tile(cos, (1, per_row)), jnp.concatenate([-sin, sin] * (per_row // 2), axis=1)


def _rms_adaln(x, gain, shift, scale):
    ms = jnp.mean(x * x, axis=-1, keepdims=True)
    return (x * lax.rsqrt(ms + NORM_EPS) * gain) * (1.0 + scale) + shift


def _head_norm_rope(y, gain, cos, sin, lane):
    low = lane < HEAD_DIM
    y2 = y * y
    s_low = jnp.sum(jnp.where(low, y2, 0.0), axis=-1, keepdims=True)
    s_high = jnp.sum(jnp.where(low, 0.0, y2), axis=-1, keepdims=True)
    ms = jnp.where(low, s_low, s_high) * (1.0 / HEAD_DIM)
    yn = y * lax.rsqrt(ms + NORM_EPS) * gain
    first_half = (lane % HEAD_DIM) < (HEAD_DIM // 2)
    partner = jnp.where(first_half,
                        pltpu.roll(yn, LANES - HEAD_DIM // 2, 1),
                        pltpu.roll(yn, HEAD_DIM // 2, 1))
    return yn * cos + partner * sin


def _inproj_kernel(x_ref, mod_ref, g_ref, w_ref, cos_ref, sin_ref, hn_ref,
                   qa_ref, ka_ref, vat_ref, qb_ref, kb_ref, vbt_ref, ga_ref, gb_ref):
    h = _rms_adaln(x_ref[...], g_ref[...], mod_ref[0:1, :], mod_ref[1:2, :])
    hb = h.astype(BF16)
    cos = cos_ref[...]
    sin = sin_ref[...]
    lane = lax.broadcasted_iota(jnp.int32, (1, LANES), 1)
    low = lane < HEAD_DIM

    def proj(i):
        return jnp.dot(hb, w_ref[:, IN_OFFSETS[i]:IN_OFFSETS[i + 1]], preferred_element_type=F32)

    def tiles(y):
        return [y[:, t * LANES:(t + 1) * LANES] for t in range(y.shape[1] // LANES)]

    def both_halves(t):
        r = pltpu.roll(t, HEAD_DIM, 1)
        return jnp.where(low, t, r), jnp.where(low, r, t)

    for t, y in enumerate(tiles(proj(0))):
        o = _head_norm_rope(y, hn_ref[0:1, :], cos, sin, lane) * Q_SCALE
        qa_ref[:, t * LANES:(t + 1) * LANES] = o.astype(BF16)
    k0, k1 = both_halves(_head_norm_rope(proj(1), hn_ref[1:2, :], cos, sin, lane))
    ka_ref[:, 0:LANES] = k0.astype(BF16)
    ka_ref[:, LANES:2 * LANES] = k1.astype(BF16)
    for t, y in enumerate(tiles(proj(3))):
        o = _head_norm_rope(y, hn_ref[2:3, :], cos, sin, lane) * Q_SCALE
        qb_ref[:, t * LANES:(t + 1) * LANES] = o.astype(BF16)
    for t, y in enumerate(tiles(proj(4))):
        o = _head_norm_rope(y, hn_ref[3:4, :], cos, sin, lane)
        kb_ref[:, t * LANES:(t + 1) * LANES] = o.astype(BF16)
    tn = (((0,), (1,)), ((), ()))
    vat_ref[0] = lax.dot_general(w_ref[:, IN_OFFSETS[2]:IN_OFFSETS[3]], hb, tn,
                                 preferred_element_type=F32).astype(BF16)
    vbt_ref[0] = lax.dot_general(w_ref[:, IN_OFFSETS[5]:IN_OFFSETS[6]], hb, tn,
                                 preferred_element_type=F32).astype(BF16)
    ga_ref[...] = jax.nn.sigmoid(proj(6)).astype(BF16)
    gb_ref[...] = jax.nn.sigmoid(proj(7)).astype(BF16)


def _inproj_call(x2d, mod, g_mix, w_in, cos, sin, head_gains, layer, seq, tm):
    tokens, d = x2d.shape
    batch = tokens // seq
    steps_per_seq = seq // tm
    row = lambda i: (i, 0)
    of_layer = lambda i: (layer, 0, 0)
    seq_major = lambda i: (i // steps_per_seq, 0, i % steps_per_seq)

    def out(width, dtype=BF16):
        return jax.ShapeDtypeStruct((tokens, width), dtype)

    return pl.pallas_call(
        _inproj_kernel,
        grid=(tokens // tm,),
        in_specs=[
            pl.BlockSpec((tm, d), row),
            pl.BlockSpec((None, None, 6, d), lambda i: (layer, i // steps_per_seq, 0, 0)),
            pl.BlockSpec((None, 1, d), of_layer),
            pl.BlockSpec((None, d, IN_W), of_layer),
            pl.BlockSpec((tm, LANES), row),
            pl.BlockSpec((tm, LANES), row),
            pl.BlockSpec((None, 4, LANES), of_layer),
        ],
        out_specs=[
            pl.BlockSpec((tm, SWA_Q_W), row),
            pl.BlockSpec((tm, 2 * LANES), row),
            pl.BlockSpec((1, SWA_KV_W, tm), seq_major),
            pl.BlockSpec((tm, MOBA_W), row),
            pl.BlockSpec((tm, MOBA_W), row),
            pl.BlockSpec((1, MOBA_W, tm), seq_major),
            pl.BlockSpec((tm, d), row),
            pl.BlockSpec((tm, d), row),
        ],
        out_shape=[
            out(SWA_Q_W), out(2 * LANES),
            jax.ShapeDtypeStruct((batch, SWA_KV_W, seq), BF16),
            out(MOBA_W), out(MOBA_W),
            jax.ShapeDtypeStruct((batch, MOBA_W, seq), BF16),
            out(d), out(d),
        ],
        compiler_params=pltpu.CompilerParams(vmem_limit_bytes=VMEM_LIMIT),
        name="in_proj",
    )(x2d, mod, g_mix, w_in, cos, sin, head_gains)


def _swa_kernel(sink_ref, q_ref, k_ref, vt_ref, o_ref, va_scr, s_scr, p_scr, tri_scr):
    W = SWA_WINDOW
    seq = q_ref.shape[1]
    group = SWA_Q_HEADS // SWA_KV_HEADS
    n_heads = HEADS_PER_TILE
    lane = lax.broadcasted_iota(jnp.int32, (1, LANES), 1)
    low = lane < HEAD_DIM
    nt = (((1,), (1,)), ((), ()))
    key = lax.broadcasted_iota(jnp.int32, (W, n_heads * W), 0)
    qry = lax.broadcasted_iota(jnp.int32, (W, n_heads * W), 1) % W
    from_prev = qry < key
    head_of_lane = lax.broadcasted_iota(jnp.int32, (1, n_heads * W), 1) // W
    tri_scr[0] = jnp.where(from_prev, 1.0, 0.0).astype(BF16)
    tri_scr[1] = jnp.where(from_prev, 0.0, 1.0).astype(BF16)

    ones = jnp.ones((SUM_ROWS, seq), BF16)
    for kvh in range(SWA_KV_HEADS):
        va_scr[kvh] = jnp.concatenate([vt_ref[0, kvh * HEAD_DIM:(kvh + 1) * HEAD_DIM, :], ones], axis=0)

    units = [(qi, t) for qi in range(seq // W) for t in range(SWA_Q_HEADS // n_heads)]
    col_max = {}

    def sink_row(t):
        s0, s1 = sink_ref[t * n_heads] * LOG2E, sink_ref[t * n_heads + 1] * LOG2E
        return jnp.where(head_of_lane == 0, s0, s1)

    def score_step(u):
        qi, t = units[u]
        kvh = (t * n_heads) // group
        q2 = q_ref[0, qi * W:(qi + 1) * W, t * LANES:(t + 1) * LANES]
        zero = jnp.zeros_like(q2)
        q_pair = jnp.concatenate([jnp.where(low, q2, zero), jnp.where(low, zero, q2)], axis=0)
        k0 = max(qi - 1, 0) * W
        s = lax.dot_general(k_ref[0, k0:k0 + 2 * W, kvh * LANES:(kvh + 1) * LANES], q_pair, nt,
                            preferred_element_type=F32)
        if qi == 0:
            s = jnp.where(from_prev, NEG_INF, s[0:W])
        else:
            s = jnp.where(from_prev, s[0:W], s[W:2 * W])
        s_scr[u % SWA_SLOTS] = s
        col_max[u] = jnp.maximum(jnp.max(s, axis=0, keepdims=True), sink_row(t))

    def exp_step(u):
        p = jnp.exp2(s_scr[u % SWA_SLOTS] - col_max[u]).astype(BF16)
        p_scr[u % SWA_SLOTS, 0:W, :] = p * tri_scr[0]
        p_scr[u % SWA_SLOTS, W:2 * W, :] = p * tri_scr[1]

    def value_step(u):
        qi, t = units[u]
        kvh = (t * n_heads) // group
        k0 = max(qi - 1, 0) * W
        if qi == 0:
            o = jnp.dot(va_scr[kvh, :, 0:W], p_scr[u % SWA_SLOTS, W:2 * W, :], preferred_element_type=F32)
        else:
            o = jnp.dot(va_scr[kvh, :, k0:k0 + 2 * W], p_scr[u % SWA_SLOTS], preferred_element_type=F32)
        denom = o[HEAD_DIM:HEAD_DIM + 1, :] + jnp.exp2(sink_row(t) - col_max.pop(u))
        res = o[0:HEAD_DIM] / denom
        tile = jnp.concatenate([res[:, j * W:(j + 1) * W] for j in range(n_heads)], axis=0)
        o_ref[0, qi * W:(qi + 1) * W, t * LANES:(t + 1) * LANES] = tile.T.astype(BF16)

    for r in range(len(units) + 2 * SWA_LAG):
        for fn, u in ((score_step, r), (exp_step, r - SWA_LAG), (value_step, r - 2 * SWA_LAG)):
            if 0 <= u < len(units):
                fn(u)


def _swa_call(sinks, qa, ka, vat):
    batch, seq, _ = qa.shape
    W = SWA_WINDOW
    assert seq % W == 0 and seq >= 2 * W and W == LANES
    return pl.pallas_call(
        _swa_kernel,
        grid=(batch,),
        in_specs=[
            pl.BlockSpec(memory_space=pltpu.SMEM),
            pl.BlockSpec((1, seq, SWA_Q_W), lambda b: (b, 0, 0)),
            pl.BlockSpec((1, seq, 2 * LANES), lambda b: (b, 0, 0)),
            pl.BlockSpec((1, SWA_KV_W, seq), lambda b: (b, 0, 0)),
        ],
        out_specs=pl.BlockSpec((1, seq, SWA_Q_W), lambda b: (b, 0, 0)),
        out_shape=jax.ShapeDtypeStruct((batch, seq, SWA_Q_W), BF16),
        scratch_shapes=[
            pltpu.VMEM((SWA_KV_HEADS, PV_ROWS, seq), BF16),
            pltpu.VMEM((SWA_SLOTS, W, HEADS_PER_TILE * W), F32),
            pltpu.VMEM((SWA_SLOTS, 2 * W, HEADS_PER_TILE * W), BF16),
            pltpu.VMEM((2, W, HEADS_PER_TILE * W), BF16),
        ],
        compiler_params=pltpu.CompilerParams(vmem_limit_bytes=VMEM_LIMIT),
        name="swa_attn",
    )(sinks, qa, ka, vat)


def _moba_offsets(col_max, col_sum, nblk):
    n_past = len(col_sum)
    keep = []
    for n in range(n_past):
        rank = jnp.where(col_sum[n] < NEG_INF, float(nblk - n_past), 0.0)
        for m in range(n_past):
            if m != n:
                before = (col_sum[m] >= col_sum[n]) if m < n else (col_sum[m] > col_sum[n])
                rank = rank + jnp.where(before, 1.0, 0.0)
        keep.append(rank < MOBA_TOPK)
    top = col_max[n_past]
    for n in range(n_past):
        top = jnp.maximum(top, jnp.where(keep[n], col_max[n], NEG_INF))
    return [jnp.where(keep[n], top, -NEG_INF) for n in range(n_past)] + [top]


def _moba_kernel(q_ref, k_ref, vt_ref, o_ref, qh_scr, va_scr, s_scr, p_scr):
    L = MOBA_BLOCK
    seq = q_ref.shape[1]
    nblk = seq // L
    lane = lax.broadcasted_iota(jnp.int32, (1, LANES), 1)
    low = lane < HEAD_DIM
    nt = (((1,), (1,)), ((), ()))
    causal_t = (lax.broadcasted_iota(jnp.int32, (L, L), 0) <= lax.broadcasted_iota(jnp.int32, (L, L), 1))

    q_all = q_ref[0]
    vt = vt_ref[0]
    ones = jnp.ones((SUM_ROWS, seq), BF16)
    for j in range(HEADS_PER_TILE):
        qh_scr[j] = jnp.where(low if j == 0 else ~low, q_all, jnp.zeros_like(q_all))
        va_scr[j] = jnp.concatenate([vt[j * HEAD_DIM:(j + 1) * HEAD_DIM], ones], axis=0)

    units = [(qi, j) for qi in range(nblk) for j in range(HEADS_PER_TILE)]
    offsets, acc, res = {}, {}, {}

    def score_step(u):
        qi, j = units[u]
        s_all = lax.dot_general(k_ref[0, 0:(qi + 1) * L, :], qh_scr[j, qi * L:(qi + 1) * L, :], nt,
                                preferred_element_type=F32)
        col_max, col_sum = [], []
        for c in range(qi + 1):
            s = s_all[c * L:(c + 1) * L]
            if c == qi:
                s = jnp.where(causal_t, s, NEG_INF)
            else:
                col_sum.append(jnp.sum(s, axis=0, keepdims=True))
            s_scr[u % MOBA_SLOTS, c * L:(c + 1) * L, :] = s
            col_max.append(jnp.max(s, axis=0, keepdims=True))
        offsets[u] = _moba_offsets(col_max, col_sum, nblk)

    def exp_step(u, c):
        p_scr[u % MOBA_SLOTS, c * L:(c + 1) * L, :] = jnp.exp2(
            s_scr[u % MOBA_SLOTS, c * L:(c + 1) * L, :] - offsets[u][c]).astype(BF16)

    def value_step(u, c):
        qi, j = units[u]
        part = jnp.dot(va_scr[j, :, c * L:(c + 1) * L], p_scr[u % MOBA_SLOTS, c * L:(c + 1) * L, :],
                       preferred_element_type=F32)
        acc[u] = part if c == 0 else acc[u] + part
        if c == qi:
            o = acc.pop(u)
            offsets.pop(u)
            res[j] = o[0:HEAD_DIM] / o[HEAD_DIM:HEAD_DIM + 1, :]
            if j == HEADS_PER_TILE - 1:
                o_ref[0, qi * L:(qi + 1) * L, :] = jnp.concatenate([res[0], res[1]], axis=0).T.astype(BF16)

    def steps(fn, u):
        if not 0 <= u < len(units):
            return []
        return [functools.partial(fn, u, c) for c in range(units[u][0] + 1)]

    for r in range(len(units) + 2 * MOBA_LAG):
        if r < len(units):
            score_step(r)
        stages = [steps(exp_step, r - MOBA_LAG), steps(value_step, r - 2 * MOBA_LAG)]
        for i in range(max(len(stage) for stage in stages)):
            for stage in stages:
                if i < len(stage):
                    stage[i]()


def _moba_call(qb, kb, vbt):
    batch, seq, width = qb.shape
    assert seq % MOBA_BLOCK == 0
    seq_spec = pl.BlockSpec((1, seq, LANES), lambda b, h: (b, 0, h))
    return pl.pallas_call(
        _moba_kernel,
        grid=(batch, width // LANES),
        in_specs=[seq_spec, seq_spec, pl.BlockSpec((1, LANES, seq), lambda b, h: (b, h, 0))],
        out_specs=seq_spec,
        out_shape=jax.ShapeDtypeStruct((batch, seq, width), BF16),
        scratch_shapes=[
            pltpu.VMEM((HEADS_PER_TILE, seq, LANES), BF16),
            pltpu.VMEM((HEADS_PER_TILE, PV_ROWS, seq), BF16),
            pltpu.VMEM((MOBA_SLOTS, seq, MOBA_BLOCK), F32),
            pltpu.VMEM((MOBA_SLOTS, seq, MOBA_BLOCK), BF16),
        ],
        compiler_params=pltpu.CompilerParams(vmem_limit_bytes=VMEM_LIMIT),
        name="moba_attn",
    )(qb, kb, vbt)


def _post_kernel(x_ref, oa_ref, ob_ref, ga_ref, gb_ref, mod_ref, g_ref,
                 woa_ref, wob_ref, wout_ref, wup_ref, wdn_ref, o_ref):
    ya = jnp.dot(oa_ref[...], woa_ref[...], preferred_element_type=F32)
    yb = jnp.dot(ob_ref[...], wob_ref[...], preferred_element_type=F32)
    mixed = ga_ref[...].astype(F32) * ya + gb_ref[...].astype(F32) * yb
    y = jnp.dot(mixed.astype(BF16), wout_ref[...], preferred_element_type=F32)
    x1 = x_ref[...] + mod_ref[2:3, :] * y
    h = _rms_adaln(x1, g_ref[...], mod_ref[3:4, :], mod_ref[4:5, :]).astype(BF16)
    acc = jnp.zeros(x1.shape, F32)
    for c in range(D_FF // FF_CHUNK):
        u = jnp.dot(h, wup_ref[:, c * FF_CHUNK:(c + 1) * FF_CHUNK], preferred_element_type=F32)
        u = jnp.square(jnp.maximum(u, 0.0)).astype(BF16)
        acc += jnp.dot(u, wdn_ref[c * FF_CHUNK:(c + 1) * FF_CHUNK, :], preferred_element_type=F32)
    o_ref[...] = x1 + mod_ref[5:6, :] * acc


def _post_call(x2d, oa, ob, ga, gb, mod, g_mlp, woa, wob, wout, wup, wdn, layer, seq, tm):
    tokens, d = x2d.shape
    steps_per_seq = seq // tm
    row = lambda i: (i, 0)
    of_layer = lambda i: (layer, 0, 0)

    def resident(w):
        return pl.BlockSpec((None,) + w.shape[1:], of_layer, pipeline_mode=pl.Buffered(1))

    return pl.pallas_call(
        _post_kernel,
        grid=(tokens // tm,),
        in_specs=[
            pl.BlockSpec((tm, d), row),
            pl.BlockSpec((tm, SWA_Q_W), row),
            pl.BlockSpec((tm, MOBA_W), row),
            pl.BlockSpec((tm, d), row),
            pl.BlockSpec((tm, d), row),
            pl.BlockSpec((None, None, 6, d), lambda i: (layer, i // steps_per_seq, 0, 0)),
            pl.BlockSpec((None, 1, d), of_layer),
            resident(woa), resident(wob), resident(wout), resident(wup), resident(wdn),
        ],
        out_specs=pl.BlockSpec((tm, d), row),
        out_shape=jax.ShapeDtypeStruct((tokens, d), F32),
        compiler_params=pltpu.CompilerParams(vmem_limit_bytes=VMEM_LIMIT),
        name="post_mlp",
    )(x2d, oa, ob, ga, gb, mod, g_mlp, woa, wob, wout, wup, wdn)


def kernel(x, c, positions, rms_mix, rms_mlp, w_ada, b_ada, w_in, q_norm_swa, k_norm_swa, q_norm_moba,
           k_norm_moba, swa_sinks, w_o_swa, w_o_moba, w_out, w_up, w_down):
    batch, seq, d = x.shape
    depth = w_in.shape[0]
    tokens = batch * seq
    assert seq % MOBA_BLOCK == 0 and d == D_MODEL

    mod = _ada_call(c, w_ada, b_ada).reshape(depth, batch, 6, d)
    cos, sin = _rope_tables(positions)
    w_in, w_o_swa, w_o_moba, w_out, w_up, w_down = (
        w.astype(BF16) for w in (w_in, w_o_swa, w_o_moba, w_out, w_up, w_down))
    gains = jnp.stack([jnp.tile(g, (1, HEADS_PER_TILE))
                       for g in (q_norm_swa, k_norm_swa, q_norm_moba, k_norm_moba)], axis=1)
    rms_mix = rms_mix.reshape(depth, 1, d)
    rms_mlp = rms_mlp.reshape(depth, 1, d)

    x2d = x.reshape(tokens, d)
    for l in range(depth):
        qa, ka, vat, qb, kb, vbt, ga, gb = _inproj_call(
            x2d, mod, rms_mix, w_in, cos, sin, gains, l, seq, tm=ROW_TILE)
        oa = _swa_call(swa_sinks[l], qa.reshape(batch, seq, -1), ka.reshape(batch, seq, -1), vat)
        ob = _moba_call(qb.reshape(batch, seq, -1), kb.reshape(batch, seq, -1), vbt)
        x2d = _post_call(x2d, oa.reshape(tokens, -1), ob.reshape(tokens, -1), ga, gb, mod, rms_mlp,
                         w_o_swa, w_o_moba, w_out, w_up, w_down, l, seq, tm=ROW_TILE)
    return x2d.reshape(batch, seq, d)
```

```python
import functools

import jax
import jax.numpy as jnp
import numpy as np
from jax import lax
from jax.experimental import pallas as pl
from jax.experimental.pallas import tpu as pltpu

D_MODEL = 1024
HEAD_DIM = 64
SWA_Q_HEADS = 8
SWA_KV_HEADS = 2
SWA_WINDOW = 128
MOBA_HEADS = 8
MOBA_BLOCK = 256
MOBA_TOPK = 3
D_FF = 4 * D_MODEL
ROPE_THETA = 10000.0
NORM_EPS = 1e-6
NEG_INF = -1e30

SWA_Q_W = SWA_Q_HEADS * HEAD_DIM
SWA_KV_W = SWA_KV_HEADS * HEAD_DIM
MOBA_W = MOBA_HEADS * HEAD_DIM
IN_SPLITS = (SWA_Q_W, SWA_KV_W, SWA_KV_W, MOBA_W, MOBA_W, MOBA_W, D_MODEL, D_MODEL)
IN_W = sum(IN_SPLITS)
IN_OFFSETS = tuple(int(o) for o in np.cumsum((0,) + IN_SPLITS))

LANES = 128
HEADS_PER_TILE = LANES // HEAD_DIM
LOG2E = 1.4426950408889634
Q_SCALE = HEAD_DIM ** -0.5 * LOG2E
SUM_ROWS = 16
PV_ROWS = HEAD_DIM + SUM_ROWS
VMEM_LIMIT = 56 * 1024 * 1024
ROW_TILE = 512
FF_CHUNK = 1024
SWA_LAG = 4
SWA_SLOTS = SWA_LAG + 1
MOBA_LAG = 2
MOBA_SLOTS = MOBA_LAG + 1

BF16 = jnp.bfloat16
F32 = jnp.float32


def _split_bf16(a):
    hi = a.astype(BF16)
    lo = (a - hi.astype(F32)).astype(BF16)
    return hi, lo


def _ada_kernel(c_ref, w_ref, b_ref, o_ref):
    c = c_ref[...]
    a = c * jax.nn.sigmoid(c)
    a_hi, a_lo = _split_bf16(a)
    w_hi, w_lo = _split_bf16(w_ref[0])
    acc = jnp.dot(a_hi, w_hi, preferred_element_type=F32)
    acc += jnp.dot(a_hi, w_lo, preferred_element_type=F32)
    acc += jnp.dot(a_lo, w_hi, preferred_element_type=F32)
    o_ref[0] = acc + b_ref[0]


def _ada_call(c, w_ada, b_ada):
    depth, d, n = w_ada.shape
    batch = c.shape[0]
    tn = 1024
    return pl.pallas_call(
        _ada_kernel,
        grid=(depth, n // tn),
        in_specs=[
            pl.BlockSpec((batch, d), lambda l, j: (0, 0)),
            pl.BlockSpec((1, d, tn), lambda l, j: (l, 0, j)),
            pl.BlockSpec((1, 1, tn), lambda l, j: (l, 0, j)),
        ],
        out_specs=pl.BlockSpec((1, batch, tn), lambda l, j: (l, 0, j)),
        out_shape=jax.ShapeDtypeStruct((depth, batch, n), F32),
        compiler_params=pltpu.CompilerParams(vmem_limit_bytes=VMEM_LIMIT),
        name="ada_mod",
    )(c, w_ada, b_ada.reshape(depth, 1, n))


def _rope_table_kernel(pos_ref, inv_ref, cos_ref, sin_ref):
    half = HEAD_DIM // 2
    rows = pos_ref.shape[1]
    ang = pos_ref[0] * inv_ref[...]
    lane = lax.broadcasted_iota(jnp.int32, (1, LANES), 1)
    sign = jnp.where((lane % HEAD_DIM) < half, -1.0, 1.0)
    for table, out_ref, scale in ((jnp.cos(ang), cos_ref, None), (jnp.sin(ang), sin_ref, sign)):
        for i in range(LANES // half):
            x = table if i == 0 else pltpu.roll(table, LANES - half * i, 1)
            x = jnp.where(lane < half, x, pltpu.roll(x, half, 1))
            x = jnp.where(lane < 2 * half, x, pltpu.roll(x, 2 * half, 1))
            out_ref[0, i * rows:(i + 1) * rows, :] = x if scale is None else x * scale


def _rope_tables(positions):
    batch, seq = positions.shape
    half = HEAD_DIM // 2
    per_row = LANES // half
    rows = seq // per_row
    inv_freq = ROPE_THETA ** (-jnp.arange(half, dtype=F32) / half)
    inv_tile = jnp.tile(inv_freq, per_row).reshape(1, LANES)
    pos = positions.astype(F32).reshape(batch, per_row, rows).transpose(0, 2, 1)
    pos = jnp.repeat(pos, half, axis=2)
    table = jax.ShapeDtypeStruct((batch, seq, LANES), F32)
    cos, sin = pl.pallas_call(
        _rope_table_kernel,
        grid=(batch,),
        in_specs=[pl.BlockSpec((1, rows, LANES), lambda b: (b, 0, 0)), pl.BlockSpec((1, LANES), lambda b: (0, 0))],
        out_specs=[pl.BlockSpec((1, seq, LANES), lambda b: (b, 0, 0))] * 2,
        out_shape=[table, table],
        name="rope_tables",
    )(pos, inv_tile)
    return cos.reshape(batch * seq, LANES), sin.reshape(batch * seq, LANES)


def _rms_adaln(x, gain, shift, scale):
    ms = jnp.mean(x * x, axis=-1, keepdims=True)
    return (x * lax.rsqrt(ms + NORM_EPS) * gain) * (1.0 + scale) + shift


def _head_norm_rope(y, gain, cos, sin, lane):
    low = lane < HEAD_DIM
    y2 = y * y
    s_low = jnp.sum(jnp.where(low, y2, 0.0), axis=-1, keepdims=True)
    s_high = jnp.sum(jnp.where(low, 0.0, y2), axis=-1, keepdims=True)
    ms = jnp.where(low, s_low, s_high) * (1.0 / HEAD_DIM)
    yn = y * lax.rsqrt(ms + NORM_EPS) * gain
    first_half = (lane % HEAD_DIM) < (HEAD_DIM // 2)
    partner = jnp.where(first_half,
                        pltpu.roll(yn, LANES - HEAD_DIM // 2, 1),
                        pltpu.roll(yn, HEAD_DIM // 2, 1))
    return yn * cos + partner * sin


def _inproj_kernel(x_ref, mod_ref, g_ref, w_ref, cos_ref, sin_ref, hn_ref,
                   qa_ref, ka_ref, vat_ref, qb_ref, kb_ref, vbt_ref, ga_ref, gb_ref):
    h = _rms_adaln(x_ref[...], g_ref[...], mod_ref[0:1, :], mod_ref[1:2, :])
    hb = h.astype(BF16)
    cos = cos_ref[...]
    sin = sin_ref[...]
    lane = lax.broadcasted_iota(jnp.int32, (1, LANES), 1)
    low = lane < HEAD_DIM

    def proj(i):
        return jnp.dot(hb, w_ref[:, IN_OFFSETS[i]:IN_OFFSETS[i + 1]], preferred_element_type=F32)

    def tiles(y):
        return [y[:, t * LANES:(t + 1) * LANES] for t in range(y.shape[1] // LANES)]

    def both_halves(t):
        r = pltpu.roll(t, HEAD_DIM, 1)
        return jnp.where(low, t, r), jnp.where(low, r, t)

    for t, y in enumerate(tiles(proj(0))):
        o = _head_norm_rope(y, hn_ref[0:1, :], cos, sin, lane) * Q_SCALE
        qa_ref[:, t * LANES:(t + 1) * LANES] = o.astype(BF16)
    k0, k1 = both_halves(_head_norm_rope(proj(1), hn_ref[1:2, :], cos, sin, lane))
    ka_ref[:, 0:LANES] = k0.astype(BF16)
    ka_ref[:, LANES:2 * LANES] = k1.astype(BF16)
    for t, y in enumerate(tiles(proj(3))):
        o = _head_norm_rope(y, hn_ref[2:3, :], cos, sin, lane) * Q_SCALE
        qb_ref[:, t * LANES:(t + 1) * LANES] = o.astype(BF16)
    for t, y in enumerate(tiles(proj(4))):
        o = _head_norm_rope(y, hn_ref[3:4, :], cos, sin, lane)
        kb_ref[:, t * LANES:(t + 1) * LANES] = o.astype(BF16)
    tn = (((0,), (1,)), ((), ()))
    vat_ref[0] = lax.dot_general(w_ref[:, IN_OFFSETS[2]:IN_OFFSETS[3]], hb, tn,
                                 preferred_element_type=F32).astype(BF16)
    vbt_ref[0] = lax.dot_general(w_ref[:, IN_OFFSETS[5]:IN_OFFSETS[6]], hb, tn,
                                 preferred_element_type=F32).astype(BF16)
    ga_ref[...] = jax.nn.sigmoid(proj(6)).astype(BF16)
    gb_ref[...] = jax.nn.sigmoid(proj(7)).astype(BF16)


def _inproj_call(x2d, mod, g_mix, w_in, cos, sin, head_gains, layer, seq, tm):
    tokens, d = x2d.shape
    batch = tokens // seq
    steps_per_seq = seq // tm
    row = lambda i: (i, 0)
    of_layer = lambda i: (layer, 0, 0)
    seq_major = lambda i: (i // steps_per_seq, 0, i % steps_per_seq)

    def out(width, dtype=BF16):
        return jax.ShapeDtypeStruct((tokens, width), dtype)

    return pl.pallas_call(
        _inproj_kernel,
        grid=(tokens // tm,),
        in_specs=[
            pl.BlockSpec((tm, d), row),
            pl.BlockSpec((None, None, 6, d), lambda i: (layer, i // steps_per_seq, 0, 0)),
            pl.BlockSpec((None, 1, d), of_layer),
            pl.BlockSpec((None, d, IN_W), of_layer, pipeline_mode=pl.Buffered(1)),
            pl.BlockSpec((tm, LANES), row),
            pl.BlockSpec((tm, LANES), row),
            pl.BlockSpec((None, 4, LANES), of_layer),
        ],
        out_specs=[
            pl.BlockSpec((tm, SWA_Q_W), row),
            pl.BlockSpec((tm, 2 * LANES), row),
            pl.BlockSpec((1, SWA_KV_W, tm), seq_major),
            pl.BlockSpec((tm, MOBA_W), row),
            pl.BlockSpec((tm, MOBA_W), row),
            pl.BlockSpec((1, MOBA_W, tm), seq_major),
            pl.BlockSpec((tm, d), row),
            pl.BlockSpec((tm, d), row),
        ],
        out_shape=[
            out(SWA_Q_W), out(2 * LANES),
            jax.ShapeDtypeStruct((batch, SWA_KV_W, seq), BF16),
            out(MOBA_W), out(MOBA_W),
            jax.ShapeDtypeStruct((batch, MOBA_W, seq), BF16),
            out(d), out(d),
        ],
        compiler_params=pltpu.CompilerParams(vmem_limit_bytes=VMEM_LIMIT),
        name="in_proj",
    )(x2d, mod, g_mix, w_in, cos, sin, head_gains)


def _swa_kernel(sink_ref, q_ref, k_ref, vt_ref, o_ref, va_scr, s_scr, p_scr, tri_scr):
    W = SWA_WINDOW
    seq = q_ref.shape[1]
    group = SWA_Q_HEADS // SWA_KV_HEADS
    n_heads = HEADS_PER_TILE
    lane = lax.broadcasted_iota(jnp.int32, (1, LANES), 1)
    low = lane < HEAD_DIM
    nt = (((1,), (1,)), ((), ()))
    key = lax.broadcasted_iota(jnp.int32, (W, n_heads * W), 0)
    qry = lax.broadcasted_iota(jnp.int32, (W, n_heads * W), 1) % W
    from_prev = qry < key
    head_of_lane = lax.broadcasted_iota(jnp.int32, (1, n_heads * W), 1) // W
    tri_scr[0] = jnp.where(from_prev, 1.0, 0.0).astype(BF16)
    tri_scr[1] = jnp.where(from_prev, 0.0, 1.0).astype(BF16)

    ones = jnp.ones((SUM_ROWS, seq), BF16)
    for kvh in range(SWA_KV_HEADS):
        va_scr[kvh] = jnp.concatenate([vt_ref[0, kvh * HEAD_DIM:(kvh + 1) * HEAD_DIM, :], ones], axis=0)

    units = [(qi, t) for qi in range(seq // W) for t in range(SWA_Q_HEADS // n_heads)]
    col_max = {}

    def sink_row(t):
        s0, s1 = sink_ref[t * n_heads] * LOG2E, sink_ref[t * n_heads + 1] * LOG2E
        return jnp.where(head_of_lane == 0, s0, s1)

    def score_step(u):
        qi, t = units[u]
        kvh = (t * n_heads) // group
        q2 = q_ref[0, qi * W:(qi + 1) * W, t * LANES:(t + 1) * LANES]
        zero = jnp.zeros_like(q2)
        q_pair = jnp.concatenate([jnp.where(low, q2, zero), jnp.where(low, zero, q2)], axis=0)
        k0 = max(qi - 1, 0) * W
        s = lax.dot_general(k_ref[0, k0:k0 + 2 * W, kvh * LANES:(kvh + 1) * LANES], q_pair, nt,
                            preferred_element_type=F32)
        if qi == 0:
            s = jnp.where(from_prev, NEG_INF, s[0:W])
        else:
            s = jnp.where(from_prev, s[0:W], s[W:2 * W])
        s_scr[u % SWA_SLOTS] = s
        col_max[u] = jnp.maximum(jnp.max(s, axis=0, keepdims=True), sink_row(t))

    def exp_step(u):
        p = jnp.exp2(s_scr[u % SWA_SLOTS] - col_max[u]).astype(BF16)
        p_scr[u % SWA_SLOTS, 0:W, :] = p * tri_scr[0]
        p_scr[u % SWA_SLOTS, W:2 * W, :] = p * tri_scr[1]

    def value_step(u):
        qi, t = units[u]
        kvh = (t * n_heads) // group
        k0 = max(qi - 1, 0) * W
        if qi == 0:
            o = jnp.dot(va_scr[kvh, :, 0:W], p_scr[u % SWA_SLOTS, W:2 * W, :], preferred_element_type=F32)
        else:
            o = jnp.dot(va_scr[kvh, :, k0:k0 + 2 * W], p_scr[u % SWA_SLOTS], preferred_element_type=F32)
        denom = o[HEAD_DIM:HEAD_DIM + 1, :] + jnp.exp2(sink_row(t) - col_max.pop(u))
        res = o[0:HEAD_DIM] / denom
        tile = jnp.concatenate([res[:, j * W:(j + 1) * W] for j in range(n_heads)], axis=0)
        o_ref[0, qi * W:(qi + 1) * W, t * LANES:(t + 1) * LANES] = tile.T.astype(BF16)

    for r in range(len(units) + 2 * SWA_LAG):
        for fn, u in ((score_step, r), (exp_step, r - SWA_LAG), (value_step, r - 2 * SWA_LAG)):
            if 0 <= u < len(units):
                fn(u)


def _swa_call(sinks, qa, ka, vat):
    batch, seq, _ = qa.shape
    W = SWA_WINDOW
    assert seq % W == 0 and seq >= 2 * W and W == LANES
    return pl.pallas_call(
        _swa_kernel,
        grid=(batch,),
        in_specs=[
            pl.BlockSpec(memory_space=pltpu.SMEM),
            pl.BlockSpec((1, seq, SWA_Q_W), lambda b: (b, 0, 0)),
            pl.BlockSpec((1, seq, 2 * LANES), lambda b: (b, 0, 0)),
            pl.BlockSpec((1, SWA_KV_W, seq), lambda b: (b, 0, 0)),
        ],
        out_specs=pl.BlockSpec((1, seq, SWA_Q_W), lambda b: (b, 0, 0)),
        out_shape=jax.ShapeDtypeStruct((batch, seq, SWA_Q_W), BF16),
        scratch_shapes=[
            pltpu.VMEM((SWA_KV_HEADS, PV_ROWS, seq), BF16),
            pltpu.VMEM((SWA_SLOTS, W, HEADS_PER_TILE * W), F32),
            pltpu.VMEM((SWA_SLOTS, 2 * W, HEADS_PER_TILE * W), BF16),
            pltpu.VMEM((2, W, HEADS_PER_TILE * W), BF16),
        ],
        compiler_params=pltpu.CompilerParams(vmem_limit_bytes=VMEM_LIMIT),
        name="swa_attn",
    )(sinks, qa, ka, vat)


def _moba_offsets(col_max, col_sum, nblk):
    n_past = len(col_sum)
    keep = []
    for n in range(n_past):
        rank = jnp.where(col_sum[n] < NEG_INF, float(nblk - n_past), 0.0)
        for m in range(n_past):
            if m != n:
                before = (col_sum[m] >= col_sum[n]) if m < n else (col_sum[m] > col_sum[n])
                rank = rank + jnp.where(before, 1.0, 0.0)
        keep.append(rank < MOBA_TOPK)
    top = col_max[n_past]
    for n in range(n_past):
        top = jnp.maximum(top, jnp.where(keep[n], col_max[n], NEG_INF))
    return [jnp.where(keep[n], top, -NEG_INF) for n in range(n_past)] + [top]


def _moba_kernel(q_ref, k_ref, vt_ref, o_ref, qh_scr, va_scr, s_scr, p_scr):
    L = MOBA_BLOCK
    seq = q_ref.shape[1]
    nblk = seq // L
    lane = lax.broadcasted_iota(jnp.int32, (1, LANES), 1)
    low = lane < HEAD_DIM
    nt = (((1,), (1,)), ((), ()))
    causal_t = (lax.broadcasted_iota(jnp.int32, (L, L), 0) <= lax.broadcasted_iota(jnp.int32, (L, L), 1))

    q_all = q_ref[0]
    vt = vt_ref[0]
    ones = jnp.ones((SUM_ROWS, seq), BF16)
    for j in range(HEADS_PER_TILE):
        qh_scr[j] = jnp.where(low if j == 0 else ~low, q_all, jnp.zeros_like(q_all))
        va_scr[j] = jnp.concatenate([vt[j * HEAD_DIM:(j + 1) * HEAD_DIM], ones], axis=0)

    units = [(qi, j) for qi in range(nblk) for j in range(HEADS_PER_TILE)]
    offsets, acc, res = {}, {}, {}

    def score_step(u):
        qi, j = units[u]
        s_all = lax.dot_general(k_ref[0, 0:(qi + 1) * L, :], qh_scr[j, qi * L:(qi + 1) * L, :], nt,
                                preferred_element_type=F32)
        col_max, col_sum = [], []
        for c in range(qi + 1):
            s = s_all[c * L:(c + 1) * L]
            if c == qi:
                s = jnp.where(causal_t, s, NEG_INF)
            else:
                col_sum.append(jnp.sum(s, axis=0, keepdims=True))
            s_scr[u % MOBA_SLOTS, c * L:(c + 1) * L, :] = s
            col_max.append(jnp.max(s, axis=0, keepdims=True))
        offsets[u] = _moba_offsets(col_max, col_sum, nblk)

    def exp_step(u, c):
        p_scr[u % MOBA_SLOTS, c * L:(c + 1) * L, :] = jnp.exp2(
            s_scr[u % MOBA_SLOTS, c * L:(c + 1) * L, :] - offsets[u][c]).astype(BF16)

    def value_step(u, c):
        qi, j = units[u]
        part = jnp.dot(va_scr[j, :, c * L:(c + 1) * L], p_scr[u % MOBA_SLOTS, c * L:(c + 1) * L, :],
                       preferred_element_type=F32)
        acc[u] = part if c == 0 else acc[u] + part
        if c == qi:
            o = acc.pop(u)
            offsets.pop(u)
            res[j] = o[0:HEAD_DIM] / o[HEAD_DIM:HEAD_DIM + 1, :]
            if j == HEADS_PER_TILE - 1:
                o_ref[0, qi * L:(qi + 1) * L, :] = jnp.concatenate([res[0], res[1]], axis=0).T.astype(BF16)

    def steps(fn, u):
        if not 0 <= u < len(units):
            return []
        return [functools.partial(fn, u, c) for c in range(units[u][0] + 1)]

    for r in range(len(units) + 2 * MOBA_LAG):
        if r < len(units):
            score_step(r)
        stages = [steps(exp_step, r - MOBA_LAG), steps(value_step, r - 2 * MOBA_LAG)]
        for i in range(max(len(stage) for stage in stages)):
            for stage in stages:
                if i < len(stage):
                    stage[i]()


def _moba_call(qb, kb, vbt):
    batch, seq, width = qb.shape
    assert seq % MOBA_BLOCK == 0
    seq_spec = pl.BlockSpec((1, seq, LANES), lambda b, h: (b, 0, h))
    return pl.pallas_call(
        _moba_kernel,
        grid=(batch, width // LANES),
        in_specs=[seq_spec, seq_spec, pl.BlockSpec((1, LANES, seq), lambda b, h: (b, h, 0))],
        out_specs=seq_spec,
        out_shape=jax.ShapeDtypeStruct((batch, seq, width), BF16),
        scratch_shapes=[
            pltpu.VMEM((HEADS_PER_TILE, seq, LANES), BF16),
            pltpu.VMEM((HEADS_PER_TILE, PV_ROWS, seq), BF16),
            pltpu.VMEM((MOBA_SLOTS, seq, MOBA_BLOCK), F32),
            pltpu.VMEM((MOBA_SLOTS, seq, MOBA_BLOCK), BF16),
        ],
        compiler_params=pltpu.CompilerParams(vmem_limit_bytes=VMEM_LIMIT),
        name="moba_attn",
    )(qb, kb, vbt)


def _post_kernel(x_ref, oa_ref, ob_ref, ga_ref, gb_ref, mod_ref, g_ref,
                 woa_ref, wob_ref, wout_ref, wup_ref, wdn_ref, o_ref):
    ya = jnp.dot(oa_ref[...], woa_ref[...], preferred_element_type=F32)
    yb = jnp.dot(ob_ref[...], wob_ref[...], preferred_element_type=F32)
    mixed = ga_ref[...].astype(F32) * ya + gb_ref[...].astype(F32) * yb
    y = jnp.dot(mixed.astype(BF16), wout_ref[...], preferred_element_type=F32)
    x1 = x_ref[...] + mod_ref[2:3, :] * y
    h = _rms_adaln(x1, g_ref[...], mod_ref[3:4, :], mod_ref[4:5, :]).astype(BF16)
    acc = jnp.zeros(x1.shape, F32)
    for c in range(D_FF // FF_CHUNK):
        u = jnp.dot(h, wup_ref[:, c * FF_CHUNK:(c + 1) * FF_CHUNK], preferred_element_type=F32)
        u = jnp.square(jnp.maximum(u, 0.0)).astype(BF16)
        acc += jnp.dot(u, wdn_ref[c * FF_CHUNK:(c + 1) * FF_CHUNK, :], preferred_element_type=F32)
    o_ref[...] = x1 + mod_ref[5:6, :] * acc


def _post_call(x2d, oa, ob, ga, gb, mod, g_mlp, woa, wob, wout, wup, wdn, layer, seq, tm):
    tokens, d = x2d.shape
    steps_per_seq = seq // tm
    row = lambda i: (i, 0)
    of_layer = lambda i: (layer, 0, 0)

    def resident(w):
        return pl.BlockSpec((None,) + w.shape[1:], of_layer, pipeline_mode=pl.Buffered(1))

    return pl.pallas_call(
        _post_kernel,
        grid=(tokens // tm,),
        in_specs=[
            pl.BlockSpec((tm, d), row),
            pl.BlockSpec((tm, SWA_Q_W), row),
            pl.BlockSpec((tm, MOBA_W), row),
            pl.BlockSpec((tm, d), row),
            pl.BlockSpec((tm, d), row),
            pl.BlockSpec((None, None, 6, d), lambda i: (layer, i // steps_per_seq, 0, 0)),
            pl.BlockSpec((None, 1, d), of_layer),
            resident(woa), resident(wob), resident(wout), resident(wup), resident(wdn),
        ],
        out_specs=pl.BlockSpec((tm, d), row),
        out_shape=jax.ShapeDtypeStruct((tokens, d), F32),
        compiler_params=pltpu.CompilerParams(vmem_limit_bytes=VMEM_LIMIT),
        name="post_mlp",
    )(x2d, oa, ob, ga, gb, mod, g_mlp, woa, wob, wout, wup, wdn)


def kernel(x, c, positions, rms_mix, rms_mlp, w_ada, b_ada, w_in, q_norm_swa, k_norm_swa, q_norm_moba,
           k_norm_moba, swa_sinks, w_o_swa, w_o_moba, w_out, w_up, w_down):
    batch, seq, d = x.shape
    depth = w_in.shape[0]
    tokens = batch * seq
    assert seq % MOBA_BLOCK == 0 and d == D_MODEL

    mod = _ada_call(c, w_ada, b_ada).reshape(depth, batch, 6, d)
    cos, sin = _rope_tables(positions)
    w_in, w_o_swa, w_o_moba, w_out, w_up, w_down = (
        w.astype(BF16) for w in (w_in, w_o_swa, w_o_moba, w_out, w_up, w_down))
    gains = jnp.stack([jnp.tile(g, (1, HEADS_PER_TILE))
                       for g in (q_norm_swa, k_norm_swa, q_norm_moba, k_norm_moba)], axis=1)
    rms_mix = rms_mix.reshape(depth, 1, d)
    rms_mlp = rms_mlp.reshape(depth, 1, d)

    x2d = x.reshape(tokens, d)
    for l in range(depth):
        qa, ka, vat, qb, kb, vbt, ga, gb = _inproj_call(
            x2d, mod, rms_mix, w_in, cos, sin, gains, l, seq, tm=2 * ROW_TILE)
        oa = _swa_call(swa_sinks[l], qa.reshape(batch, seq, -1), ka.reshape(batch, seq, -1), vat)
        ob = _moba_call(qb.reshape(batch, seq, -1), kb.reshape(batch, seq, -1), vbt)
        x2d = _post_call(x2d, oa.reshape(tokens, -1), ob.reshape(tokens, -1), ga, gb, mod, rms_mlp,
                         w_o_swa, w_o_moba, w_out, w_up, w_down, l, seq, tm=ROW_TILE)
    return x2d.reshape(batch, seq, d)
```

```python
import functools

import jax
import jax.numpy as jnp
import numpy as np
from jax import lax
from jax.experimental import pallas as pl
from jax.experimental.pallas import tpu as pltpu

D_MODEL = 1024
HEAD_DIM = 64
SWA_Q_HEADS = 8
SWA_KV_HEADS = 2
SWA_WINDOW = 128
MOBA_HEADS = 8
MOBA_BLOCK = 256
MOBA_TOPK = 3
D_FF = 4 * D_MODEL
ROPE_THETA = 10000.0
NORM_EPS = 1e-6
NEG_INF = -1e30

SWA_Q_W = SWA_Q_HEADS * HEAD_DIM
SWA_KV_W = SWA_KV_HEADS * HEAD_DIM
MOBA_W = MOBA_HEADS * HEAD_DIM
IN_SPLITS = (SWA_Q_W, SWA_KV_W, SWA_KV_W, MOBA_W, MOBA_W, MOBA_W, D_MODEL, D_MODEL)
IN_W = sum(IN_SPLITS)
IN_OFFSETS = tuple(int(o) for o in np.cumsum((0,) + IN_SPLITS))

LANES = 128
HEADS_PER_TILE = LANES // HEAD_DIM
LOG2E = 1.4426950408889634
Q_SCALE = HEAD_DIM ** -0.5 * LOG2E
SUM_ROWS = 16
PV_ROWS = HEAD_DIM + SUM_ROWS
VMEM_LIMIT = 56 * 1024 * 1024
ROW_TILE = 512
FF_CHUNK = 1024
SWA_LAG = 4
SWA_SLOTS = SWA_LAG + 1
MOBA_LAG = 2
MOBA_SLOTS = MOBA_LAG + 1

BF16 = jnp.bfloat16
F32 = jnp.float32


def _split_bf16(a):
    hi = a.astype(BF16)
    lo = (a - hi.astype(F32)).astype(BF16)
    return hi, lo


def _ada_kernel(c_ref, w_ref, b_ref, o_ref):
    c = c_ref[...]
    a = c * jax.nn.sigmoid(c)
    a_hi, a_lo = _split_bf16(a)
    w_hi, w_lo = _split_bf16(w_ref[0])
    acc = jnp.dot(a_hi, w_hi, preferred_element_type=F32)
    acc += jnp.dot(a_hi, w_lo, preferred_element_type=F32)
    acc += jnp.dot(a_lo, w_hi, preferred_element_type=F32)
    o_ref[0] = acc + b_ref[0]


def _ada_call(c, w_ada, b_ada):
    depth, d, n = w_ada.shape
    batch = c.shape[0]
    tn = 1024
    return pl.pallas_call(
        _ada_kernel,
        grid=(depth, n // tn),
        in_specs=[
            pl.BlockSpec((batch, d), lambda l, j: (0, 0)),
            pl.BlockSpec((1, d, tn), lambda l, j: (l, 0, j)),
            pl.BlockSpec((1, 1, tn), lambda l, j: (l, 0, j)),
        ],
        out_specs=pl.BlockSpec((1, batch, tn), lambda l, j: (l, 0, j)),
        out_shape=jax.ShapeDtypeStruct((depth, batch, n), F32),
        compiler_params=pltpu.CompilerParams(vmem_limit_bytes=VMEM_LIMIT),
        name="ada_mod",
    )(c, w_ada, b_ada.reshape(depth, 1, n))


def _rope_table_kernel(pos_ref, inv_ref, cos_ref, sin_ref):
    half = HEAD_DIM // 2
    rows = pos_ref.shape[1]
    ang = pos_ref[0] * inv_ref[...]
    lane = lax.broadcasted_iota(jnp.int32, (1, LANES), 1)
    sign = jnp.where((lane % HEAD_DIM) < half, -1.0, 1.0)
    for table, out_ref, scale in ((jnp.cos(ang), cos_ref, None), (jnp.sin(ang), sin_ref, sign)):
        for i in range(LANES // half):
            x = table if i == 0 else pltpu.roll(table, LANES - half * i, 1)
            x = jnp.where(lane < half, x, pltpu.roll(x, half, 1))
            x = jnp.where(lane < 2 * half, x, pltpu.roll(x, 2 * half, 1))
            out_ref[0, i * rows:(i + 1) * rows, :] = x if scale is None else x * scale


def _rope_tables(positions):
    batch, seq = positions.shape
    half = HEAD_DIM // 2
    per_row = LANES // half
    rows = seq // per_row
    inv_freq = ROPE_THETA ** (-jnp.arange(half, dtype=F32) / half)
    inv_tile = jnp.tile(inv_freq, per_row).reshape(1, LANES)
    pos = positions.astype(F32).reshape(batch, per_row, rows).transpose(0, 2, 1)
    pos = jnp.repeat(pos, half, axis=2)
    table = jax.ShapeDtypeStruct((batch, seq, LANES), F32)
    cos, sin = pl.pallas_call(
        _rope_table_kernel,
        grid=(batch,),
        in_specs=[pl.BlockSpec((1, rows, LANES), lambda b: (b, 0, 0)), pl.BlockSpec((1, LANES), lambda b: (0, 0))],
        out_specs=[pl.BlockSpec((1, seq, LANES), lambda b: (b, 0, 0))] * 2,
        out_shape=[table, table],
        name="rope_tables",
    )(pos, inv_tile)
    return cos.reshape(batch * seq, LANES), sin.reshape(batch * seq, LANES)


def _rms_adaln(x, gain, shift, scale):
    ms = jnp.mean(x * x, axis=-1, keepdims=True)
    return (x * lax.rsqrt(ms + NORM_EPS) * gain) * (1.0 + scale) + shift


def _head_norm_rope(y, gain, cos, sin, lane):
    low = lane < HEAD_DIM
    y2 = y * y
    s_low = jnp.sum(jnp.where(low, y2, 0.0), axis=-1, keepdims=True)
    s_high = jnp.sum(jnp.where(low, 0.0, y2), axis=-1, keepdims=True)
    ms = jnp.where(low, s_low, s_high) * (1.0 / HEAD_DIM)
    yn = y * lax.rsqrt(ms + NORM_EPS) * gain
    first_half = (lane % HEAD_DIM) < (HEAD_DIM // 2)
    partner = jnp.where(first_half,
                        pltpu.roll(yn, LANES - HEAD_DIM // 2, 1),
                        pltpu.roll(yn, HEAD_DIM // 2, 1))
    return yn * cos + partner * sin


def _inproj_kernel(x_ref, mod_ref, g_ref, w_ref, cos_ref, sin_ref, hn_ref,
                   qa_ref, ka_ref, vat_ref, qb_ref, kb_ref, vbt_ref, ga_ref, gb_ref):
    h = _rms_adaln(x_ref[...], g_ref[...], mod_ref[0:1, :], mod_ref[1:2, :])
    hb = h.astype(BF16)
    cos = cos_ref[...]
    sin = sin_ref[...]
    lane = lax.broadcasted_iota(jnp.int32, (1, LANES), 1)
    low = lane < HEAD_DIM

    def proj(i):
        return jnp.dot(hb, w_ref[:, IN_OFFSETS[i]:IN_OFFSETS[i + 1]], preferred_element_type=F32)

    def tiles(y):
        return [y[:, t * LANES:(t + 1) * LANES] for t in range(y.shape[1] // LANES)]

    def both_halves(t):
        r = pltpu.roll(t, HEAD_DIM, 1)
        return jnp.where(low, t, r), jnp.where(low, r, t)

    def heads(i, gain_row, out_ref, scale=None):
        for t, y in enumerate(tiles(proj(i))):
            o = _head_norm_rope(y, hn_ref[gain_row:gain_row + 1, :], cos, sin, lane)
            o = o if scale is None else o * scale
            out_ref[:, t * LANES:(t + 1) * LANES] = o.astype(BF16)

    def swa_k():
        k0, k1 = both_halves(_head_norm_rope(proj(1), hn_ref[1:2, :], cos, sin, lane))
        ka_ref[:, 0:LANES] = k0.astype(BF16)
        ka_ref[:, LANES:2 * LANES] = k1.astype(BF16)

    def values():
        tn = (((0,), (1,)), ((), ()))
        for i, out_ref in ((2, vat_ref), (5, vbt_ref)):
            out_ref[0] = lax.dot_general(w_ref[:, IN_OFFSETS[i]:IN_OFFSETS[i + 1]], hb, tn,
                                         preferred_element_type=F32).astype(BF16)

    def gate(i, out_ref, half):
        cols = slice(half * (D_MODEL // 2), (half + 1) * (D_MODEL // 2))
        y = jnp.dot(hb, w_ref[:, IN_OFFSETS[i] + cols.start:IN_OFFSETS[i] + cols.stop],
                    preferred_element_type=F32)
        out_ref[:, cols] = (0.5 * jnp.tanh(0.5 * y) + 0.5).astype(BF16)

    heads(0, 0, qa_ref, Q_SCALE)
    swa_k()
    heads(3, 2, qb_ref, Q_SCALE)
    heads(4, 3, kb_ref)
    values()
    gate(6, ga_ref, 0)
    gate(6, ga_ref, 1)
    gate(7, gb_ref, 0)
    gate(7, gb_ref, 1)


def _inproj_call(x2d, mod, g_mix, w_in, cos, sin, head_gains, layer, seq, tm):
    tokens, d = x2d.shape
    batch = tokens // seq
    steps_per_seq = seq // tm
    row = lambda i: (i, 0)
    of_layer = lambda i: (layer, 0, 0)
    seq_major = lambda i: (i // steps_per_seq, 0, i % steps_per_seq)

    def out(width, dtype=BF16):
        return jax.ShapeDtypeStruct((tokens, width), dtype)

    return pl.pallas_call(
        _inproj_kernel,
        grid=(tokens // tm,),
        in_specs=[
            pl.BlockSpec((tm, d), row),
            pl.BlockSpec((None, None, 6, d), lambda i: (layer, i // steps_per_seq, 0, 0)),
            pl.BlockSpec((None, 1, d), of_layer),
            pl.BlockSpec((None, d, IN_W), of_layer, pipeline_mode=pl.Buffered(1)),
            pl.BlockSpec((tm, LANES), row),
            pl.BlockSpec((tm, LANES), row),
            pl.BlockSpec((None, 4, LANES), of_layer),
        ],
        out_specs=[
            pl.BlockSpec((tm, SWA_Q_W), row),
            pl.BlockSpec((tm, 2 * LANES), row),
            pl.BlockSpec((1, SWA_KV_W, tm), seq_major),
            pl.BlockSpec((tm, MOBA_W), row),
            pl.BlockSpec((tm, MOBA_W), row),
            pl.BlockSpec((1, MOBA_W, tm), seq_major),
            pl.BlockSpec((tm, d), row),
            pl.BlockSpec((tm, d), row),
        ],
        out_shape=[
            out(SWA_Q_W), out(2 * LANES),
            jax.ShapeDtypeStruct((batch, SWA_KV_W, seq), BF16),
            out(MOBA_W), out(MOBA_W),
            jax.ShapeDtypeStruct((batch, MOBA_W, seq), BF16),
            out(d), out(d),
        ],
        compiler_params=pltpu.CompilerParams(vmem_limit_bytes=VMEM_LIMIT),
        name="in_proj",
    )(x2d, mod, g_mix, w_in, cos, sin, head_gains)


def _swa_kernel(sink_ref, q_ref, k_ref, vt_ref, o_ref, va_scr, s_scr, p_scr, tri_scr):
    W = SWA_WINDOW
    seq = q_ref.shape[1]
    group = SWA_Q_HEADS // SWA_KV_HEADS
    n_heads = HEADS_PER_TILE
    lane = lax.broadcasted_iota(jnp.int32, (1, LANES), 1)
    low = lane < HEAD_DIM
    nt = (((1,), (1,)), ((), ()))
    key = lax.broadcasted_iota(jnp.int32, (W, n_heads * W), 0)
    qry = lax.broadcasted_iota(jnp.int32, (W, n_heads * W), 1) % W
    from_prev = qry < key
    head_of_lane = lax.broadcasted_iota(jnp.int32, (1, n_heads * W), 1) // W
    tri_scr[0] = jnp.where(from_prev, 1.0, 0.0).astype(BF16)
    tri_scr[1] = jnp.where(from_prev, 0.0, 1.0).astype(BF16)

    ones = jnp.ones((SUM_ROWS, seq), BF16)
    for kvh in range(SWA_KV_HEADS):
        va_scr[kvh] = jnp.concatenate([vt_ref[0, kvh * HEAD_DIM:(kvh + 1) * HEAD_DIM, :], ones], axis=0)

    units = [(qi, t) for qi in range(seq // W) for t in range(SWA_Q_HEADS // n_heads)]
    col_max = {}

    def sink_row(t):
        s0, s1 = sink_ref[t * n_heads] * LOG2E, sink_ref[t * n_heads + 1] * LOG2E
        return jnp.where(head_of_lane == 0, s0, s1)

    def score_step(u):
        qi, t = units[u]
        kvh = (t * n_heads) // group
        q2 = q_ref[0, qi * W:(qi + 1) * W, t * LANES:(t + 1) * LANES]
        zero = jnp.zeros_like(q2)
        q_pair = jnp.concatenate([jnp.where(low, q2, zero), jnp.where(low, zero, q2)], axis=0)
        k0 = max(qi - 1, 0) * W
        s = lax.dot_general(k_ref[0, k0:k0 + 2 * W, kvh * LANES:(kvh + 1) * LANES], q_pair, nt,
                            preferred_element_type=F32)
        if qi == 0:
            s = jnp.where(from_prev, NEG_INF, s[0:W])
        else:
            s = jnp.where(from_prev, s[0:W], s[W:2 * W])
        s_scr[u % SWA_SLOTS] = s
        col_max[u] = jnp.maximum(jnp.max(s, axis=0, keepdims=True), sink_row(t))

    def exp_step(u):
        p = jnp.exp2(s_scr[u % SWA_SLOTS] - col_max[u]).astype(BF16)
        p_scr[u % SWA_SLOTS, 0:W, :] = p * tri_scr[0]
        p_scr[u % SWA_SLOTS, W:2 * W, :] = p * tri_scr[1]

    def value_step(u):
        qi, t = units[u]
        kvh = (t * n_heads) // group
        k0 = max(qi - 1, 0) * W
        if qi == 0:
            o = jnp.dot(va_scr[kvh, :, 0:W], p_scr[u % SWA_SLOTS, W:2 * W, :], preferred_element_type=F32)
        else:
            o = jnp.dot(va_scr[kvh, :, k0:k0 + 2 * W], p_scr[u % SWA_SLOTS], preferred_element_type=F32)
        denom = o[HEAD_DIM:HEAD_DIM + 1, :] + jnp.exp2(sink_row(t) - col_max.pop(u))
        res = o[0:HEAD_DIM] / denom
        tile = jnp.concatenate([res[:, j * W:(j + 1) * W] for j in range(n_heads)], axis=0)
        o_ref[0, qi * W:(qi + 1) * W, t * LANES:(t + 1) * LANES] = tile.T.astype(BF16)

    for r in range(len(units) + 2 * SWA_LAG):
        for fn, u in ((score_step, r), (exp_step, r - SWA_LAG), (value_step, r - 2 * SWA_LAG)):
            if 0 <= u < len(units):
                fn(u)


def _swa_call(sinks, qa, ka, vat):
    batch, seq, _ = qa.shape
    W = SWA_WINDOW
    assert seq % W == 0 and seq >= 2 * W and W == LANES
    return pl.pallas_call(
        _swa_kernel,
        grid=(batch,),
        in_specs=[
            pl.BlockSpec(memory_space=pltpu.SMEM),
            pl.BlockSpec((1, seq, SWA_Q_W), lambda b: (b, 0, 0)),
            pl.BlockSpec((1, seq, 2 * LANES), lambda b: (b, 0, 0)),
            pl.BlockSpec((1, SWA_KV_W, seq), lambda b: (b, 0, 0)),
        ],
        out_specs=pl.BlockSpec((1, seq, SWA_Q_W), lambda b: (b, 0, 0)),
        out_shape=jax.ShapeDtypeStruct((batch, seq, SWA_Q_W), BF16),
        scratch_shapes=[
            pltpu.VMEM((SWA_KV_HEADS, PV_ROWS, seq), BF16),
            pltpu.VMEM((SWA_SLOTS, W, HEADS_PER_TILE * W), F32),
            pltpu.VMEM((SWA_SLOTS, 2 * W, HEADS_PER_TILE * W), BF16),
            pltpu.VMEM((2, W, HEADS_PER_TILE * W), BF16),
        ],
        compiler_params=pltpu.CompilerParams(vmem_limit_bytes=VMEM_LIMIT),
        name="swa_attn",
    )(sinks, qa, ka, vat)


def _moba_offsets(col_max, col_sum, nblk):
    n_past = len(col_sum)
    keep = []
    for n in range(n_past):
        rank = jnp.where(col_sum[n] < NEG_INF, float(nblk - n_past), 0.0)
        for m in range(n_past):
            if m != n:
                before = (col_sum[m] >= col_sum[n]) if m < n else (col_sum[m] > col_sum[n])
                rank = rank + jnp.where(before, 1.0, 0.0)
        keep.append(rank < MOBA_TOPK)
    top = col_max[n_past]
    for n in range(n_past):
        top = jnp.maximum(top, jnp.where(keep[n], col_max[n], NEG_INF))
    return [jnp.where(keep[n], top, -NEG_INF) for n in range(n_past)] + [top]


def _moba_kernel(q_ref, k_ref, vt_ref, o_ref, qh_scr, va_scr, s_scr, p_scr):
    L = MOBA_BLOCK
    seq = q_ref.shape[1]
    nblk = seq // L
    lane = lax.broadcasted_iota(jnp.int32, (1, LANES), 1)
    low = lane < HEAD_DIM
    nt = (((1,), (1,)), ((), ()))
    causal_t = (lax.broadcasted_iota(jnp.int32, (L, L), 0) <= lax.broadcasted_iota(jnp.int32, (L, L), 1))

    q_all = q_ref[0]
    vt = vt_ref[0]
    ones = jnp.ones((SUM_ROWS, seq), BF16)
    for j in range(HEADS_PER_TILE):
        qh_scr[j] = jnp.where(low if j == 0 else ~low, q_all, jnp.zeros_like(q_all))
        va_scr[j] = jnp.concatenate([vt[j * HEAD_DIM:(j + 1) * HEAD_DIM], ones], axis=0)

    units = [(qi, j) for qi in range(nblk) for j in range(HEADS_PER_TILE)]
    offsets, acc, res = {}, {}, {}

    def score_step(u):
        qi, j = units[u]
        s_all = lax.dot_general(k_ref[0, 0:(qi + 1) * L, :], qh_scr[j, qi * L:(qi + 1) * L, :], nt,
                                preferred_element_type=F32)
        col_max, col_sum = [], []
        for c in range(qi + 1):
            s = s_all[c * L:(c + 1) * L]
            if c == qi:
                s = jnp.where(causal_t, s, NEG_INF)
            else:
                col_sum.append(jnp.sum(s, axis=0, keepdims=True))
            s_scr[u % MOBA_SLOTS, c * L:(c + 1) * L, :] = s
            col_max.append(jnp.max(s, axis=0, keepdims=True))
        offsets[u] = _moba_offsets(col_max, col_sum, nblk)

    def exp_step(u, c):
        p_scr[u % MOBA_SLOTS, c * L:(c + 1) * L, :] = jnp.exp2(
            s_scr[u % MOBA_SLOTS, c * L:(c + 1) * L, :] - offsets[u][c]).astype(BF16)

    def value_step(u, c):
        qi, j = units[u]
        part = jnp.dot(va_scr[j, :, c * L:(c + 1) * L], p_scr[u % MOBA_SLOTS, c * L:(c + 1) * L, :],
                       preferred_element_type=F32)
        acc[u] = part if c == 0 else acc[u] + part
        if c == qi:
            o = acc.pop(u)
            offsets.pop(u)
            res[j] = o[0:HEAD_DIM] / o[HEAD_DIM:HEAD_DIM + 1, :]
            if j == HEADS_PER_TILE - 1:
                o_ref[0, qi * L:(qi + 1) * L, :] = jnp.concatenate([res[0], res[1]], axis=0).T.astype(BF16)

    def steps(fn, u):
        if not 0 <= u < len(units):
            return []
        return [functools.partial(fn, u, c) for c in range(units[u][0] + 1)]

    for r in range(len(units) + 2 * MOBA_LAG):
        if r < len(units):
            score_step(r)
        stages = [steps(exp_step, r - MOBA_LAG), steps(value_step, r - 2 * MOBA_LAG)]
        for i in range(max(len(stage) for stage in stages)):
            for stage in stages:
                if i < len(stage):
                    stage[i]()


def _moba_call(qb, kb, vbt):
    batch, seq, width = qb.shape
    assert seq % MOBA_BLOCK == 0
    seq_spec = pl.BlockSpec((1, seq, LANES), lambda b, h: (b, 0, h))
    return pl.pallas_call(
        _moba_kernel,
        grid=(batch, width // LANES),
        in_specs=[seq_spec, seq_spec, pl.BlockSpec((1, LANES, seq), lambda b, h: (b, h, 0))],
        out_specs=seq_spec,
        out_shape=jax.ShapeDtypeStruct((batch, seq, width), BF16),
        scratch_shapes=[
            pltpu.VMEM((HEADS_PER_TILE, seq, LANES), BF16),
            pltpu.VMEM((HEADS_PER_TILE, PV_ROWS, seq), BF16),
            pltpu.VMEM((MOBA_SLOTS, seq, MOBA_BLOCK), F32),
            pltpu.VMEM((MOBA_SLOTS, seq, MOBA_BLOCK), BF16),
        ],
        compiler_params=pltpu.CompilerParams(vmem_limit_bytes=VMEM_LIMIT),
        name="moba_attn",
    )(qb, kb, vbt)


def _post_kernel(x_ref, oa_ref, ob_ref, ga_ref, gb_ref, mod_ref, g_ref,
                 woa_ref, wob_ref, wout_ref, wup_ref, wdn_ref, o_ref):
    ya = jnp.dot(oa_ref[...], woa_ref[...], preferred_element_type=F32)
    yb = jnp.dot(ob_ref[...], wob_ref[...], preferred_element_type=F32)
    mixed = ga_ref[...].astype(F32) * ya + gb_ref[...].astype(F32) * yb
    y = jnp.dot(mixed.astype(BF16), wout_ref[...], preferred_element_type=F32)
    x1 = x_ref[...] + mod_ref[2:3, :] * y
    h = _rms_adaln(x1, g_ref[...], mod_ref[3:4, :], mod_ref[4:5, :]).astype(BF16)
    acc = jnp.zeros(x1.shape, F32)
    for c in range(D_FF // FF_CHUNK):
        u = jnp.dot(h, wup_ref[:, c * FF_CHUNK:(c + 1) * FF_CHUNK], preferred_element_type=F32)
        u = jnp.square(jnp.maximum(u, 0.0)).astype(BF16)
        acc += jnp.dot(u, wdn_ref[c * FF_CHUNK:(c + 1) * FF_CHUNK, :], preferred_element_type=F32)
    o_ref[...] = x1 + mod_ref[5:6, :] * acc


def _post_call(x2d, oa, ob, ga, gb, mod, g_mlp, woa, wob, wout, wup, wdn, layer, seq, tm):
    tokens, d = x2d.shape
    steps_per_seq = seq // tm
    row = lambda i: (i, 0)
    of_layer = lambda i: (layer, 0, 0)

    def resident(w):
        return pl.BlockSpec((None,) + w.shape[1:], of_layer, pipeline_mode=pl.Buffered(1))

    return pl.pallas_call(
        _post_kernel,
        grid=(tokens // tm,),
        in_specs=[
            pl.BlockSpec((tm, d), row),
            pl.BlockSpec((tm, SWA_Q_W), row),
            pl.BlockSpec((tm, MOBA_W), row),
            pl.BlockSpec((tm, d), row),
            pl.BlockSpec((tm, d), row),
            pl.BlockSpec((None, None, 6, d), lambda i: (layer, i // steps_per_seq, 0, 0)),
            pl.BlockSpec((None, 1, d), of_layer),
            resident(woa), resident(wob), resident(wout), resident(wup), resident(wdn),
        ],
        out_specs=pl.BlockSpec((tm, d), row),
        out_shape=jax.ShapeDtypeStruct((tokens, d), F32),
        compiler_params=pltpu.CompilerParams(vmem_limit_bytes=VMEM_LIMIT),
        name="post_mlp",
    )(x2d, oa, ob, ga, gb, mod, g_mlp, woa, wob, wout, wup, wdn)


def kernel(x, c, positions, rms_mix, rms_mlp, w_ada, b_ada, w_in, q_norm_swa, k_norm_swa, q_norm_moba,
           k_norm_moba, swa_sinks, w_o_swa, w_o_moba, w_out, w_up, w_down):
    batch, seq, d = x.shape
    depth = w_in.shape[0]
    tokens = batch * seq
    assert seq % MOBA_BLOCK == 0 and d == D_MODEL

    mod = _ada_call(c, w_ada, b_ada).reshape(depth, batch, 6, d)
    cos, sin = _rope_tables(positions)
    w_in, w_o_swa, w_o_moba, w_out, w_up, w_down = (
        w.astype(BF16) for w in (w_in, w_o_swa, w_o_moba, w_out, w_up, w_down))
    gains = jnp.stack([jnp.tile(g, (1, HEADS_PER_TILE))
                       for g in (q_norm_swa, k_norm_swa, q_norm_moba, k_norm_moba)], axis=1)
    rms_mix = rms_mix.reshape(depth, 1, d)
    rms_mlp = rms_mlp.reshape(depth, 1, d)

    x2d = x.reshape(tokens, d)
    for l in range(depth):
        qa, ka, vat, qb, kb, vbt, ga, gb = _inproj_call(
            x2d, mod, rms_mix, w_in, cos, sin, gains, l, seq, tm=2 * ROW_TILE)
        oa = _swa_call(swa_sinks[l], qa.reshape(batch, seq, -1), ka.reshape(batch, seq, -1), vat)
        ob = _moba_call(qb.reshape(batch, seq, -1), kb.reshape(batch, seq, -1), vbt)
        x2d = _post_call(x2d, oa.reshape(tokens, -1), ob.reshape(tokens, -1), ga, gb, mod, rms_mlp,
                         w_o_swa, w_o_moba, w_out, w_up, w_down, l, seq, tm=ROW_TILE)
    return x2d.reshape(batch, seq, d)
```

```python
import functools

import jax
import jax.numpy as jnp
import numpy as np
from jax import lax
from jax.experimental import pallas as pl
from jax.experimental.pallas import tpu as pltpu

D_MODEL = 1024
HEAD_DIM = 64
SWA_Q_HEADS = 8
SWA_KV_HEADS = 2
SWA_WINDOW = 128
MOBA_HEADS = 8
MOBA_BLOCK = 256
MOBA_TOPK = 3
D_FF = 4 * D_MODEL
ROPE_THETA = 10000.0
NORM_EPS = 1e-6
NEG_INF = -1e30

SWA_Q_W = SWA_Q_HEADS * HEAD_DIM
SWA_KV_W = SWA_KV_HEADS * HEAD_DIM
MOBA_W = MOBA_HEADS * HEAD_DIM
IN_SPLITS = (SWA_Q_W, SWA_KV_W, SWA_KV_W, MOBA_W, MOBA_W, MOBA_W, D_MODEL, D_MODEL)
IN_W = sum(IN_SPLITS)
IN_OFFSETS = tuple(int(o) for o in np.cumsum((0,) + IN_SPLITS))

LANES = 128
HEADS_PER_TILE = LANES // HEAD_DIM
ROPE_HALF = HEAD_DIM // 2
LOG2E = 1.4426950408889634
Q_SCALE = HEAD_DIM ** -0.5 * LOG2E
SUM_ROWS = 16
PV_ROWS = HEAD_DIM + SUM_ROWS
VMEM_LIMIT = 56 * 1024 * 1024
ROW_TILE = 512
FF_CHUNK = 1024
SWA_LAG = 4
SWA_SLOTS = SWA_LAG + 1
MOBA_LAG = 2
MOBA_SLOTS = MOBA_LAG + 1

BF16 = jnp.bfloat16
F32 = jnp.float32


def _is_head0(lane):
    return (lane // ROPE_HALF) % HEADS_PER_TILE == 0


def _pair_rope_halves(w):
    lead, width = w.shape[:-1], w.shape[-1]
    w = w.reshape(*lead, width // LANES, HEADS_PER_TILE, 2, ROPE_HALF)
    return jnp.swapaxes(w, -2, -3).reshape(*lead, width)


def _split_bf16(a):
    hi = a.astype(BF16)
    lo = (a - hi.astype(F32)).astype(BF16)
    return hi, lo


def _ada_kernel(c_ref, w_ref, b_ref, o_ref):
    c = c_ref[...]
    a = c * jax.nn.sigmoid(c)
    a_hi, a_lo = _split_bf16(a)
    w_hi, w_lo = _split_bf16(w_ref[0])
    acc = jnp.dot(a_hi, w_hi, preferred_element_type=F32)
    acc += jnp.dot(a_hi, w_lo, preferred_element_type=F32)
    acc += jnp.dot(a_lo, w_hi, preferred_element_type=F32)
    o_ref[0] = acc + b_ref[0]


def _ada_call(c, w_ada, b_ada):
    depth, d, n = w_ada.shape
    batch = c.shape[0]
    tn = 1024
    return pl.pallas_call(
        _ada_kernel,
        grid=(depth, n // tn),
        in_specs=[
            pl.BlockSpec((batch, d), lambda l, j: (0, 0)),
            pl.BlockSpec((1, d, tn), lambda l, j: (l, 0, j)),
            pl.BlockSpec((1, 1, tn), lambda l, j: (l, 0, j)),
        ],
        out_specs=pl.BlockSpec((1, batch, tn), lambda l, j: (l, 0, j)),
        out_shape=jax.ShapeDtypeStruct((depth, batch, n), F32),
        compiler_params=pltpu.CompilerParams(vmem_limit_bytes=VMEM_LIMIT),
        name="ada_mod",
    )(c, w_ada, b_ada.reshape(depth, 1, n))


def _rope_table_kernel(pos_ref, inv_ref, cos_ref, sin_ref):
    half = HEAD_DIM // 2
    rows = pos_ref.shape[1]
    ang = pos_ref[0] * inv_ref[...]
    lane = lax.broadcasted_iota(jnp.int32, (1, LANES), 1)
    sign = jnp.where(lane < LANES // 2, -1.0, 1.0)
    for table, out_ref, scale in ((jnp.cos(ang), cos_ref, None), (jnp.sin(ang), sin_ref, sign)):
        for i in range(LANES // half):
            x = table if i == 0 else pltpu.roll(table, LANES - half * i, 1)
            x = jnp.where(lane < half, x, pltpu.roll(x, half, 1))
            x = jnp.where(lane < 2 * half, x, pltpu.roll(x, 2 * half, 1))
            out_ref[0, i * rows:(i + 1) * rows, :] = x if scale is None else x * scale


def _rope_tables(positions):
    batch, seq = positions.shape
    half = HEAD_DIM // 2
    per_row = LANES // half
    rows = seq // per_row
    inv_freq = ROPE_THETA ** (-jnp.arange(half, dtype=F32) / half)
    inv_tile = jnp.tile(inv_freq, per_row).reshape(1, LANES)
    pos = positions.astype(F32).reshape(batch, per_row, rows).transpose(0, 2, 1)
    pos = jnp.repeat(pos, half, axis=2)
    table = jax.ShapeDtypeStruct((batch, seq, LANES), F32)
    cos, sin = pl.pallas_call(
        _rope_table_kernel,
        grid=(batch,),
        in_specs=[pl.BlockSpec((1, rows, LANES), lambda b: (b, 0, 0)), pl.BlockSpec((1, LANES), lambda b: (0, 0))],
        out_specs=[pl.BlockSpec((1, seq, LANES), lambda b: (b, 0, 0))] * 2,
        out_shape=[table, table],
        name="rope_tables",
    )(pos, inv_tile)
    return cos.reshape(batch * seq, LANES), sin.reshape(batch * seq, LANES)


def _rms_adaln(x, gain, shift, scale):
    ms = jnp.mean(x * x, axis=-1, keepdims=True)
    return (x * lax.rsqrt(ms + NORM_EPS) * gain) * (1.0 + scale) + shift


def _head_norm_rope(y, gain, cos, sin, lane):
    head0 = _is_head0(lane)
    y2 = y * y
    s0 = jnp.sum(jnp.where(head0, y2, 0.0), axis=-1, keepdims=True)
    s1 = jnp.sum(jnp.where(head0, 0.0, y2), axis=-1, keepdims=True)
    ms = jnp.where(head0, s0, s1) * (1.0 / HEAD_DIM)
    yn = y * lax.rsqrt(ms + NORM_EPS) * gain
    return yn * cos + pltpu.roll(yn, LANES // 2, 1) * sin


def _inproj_kernel(x_ref, mod_ref, g_ref, w_ref, cos_ref, sin_ref, hn_ref,
                   qa_ref, ka_ref, vat_ref, qb_ref, kb_ref, vbt_ref, ga_ref, gb_ref):
    h = _rms_adaln(x_ref[...], g_ref[...], mod_ref[0:1, :], mod_ref[1:2, :])
    hb = h.astype(BF16)
    cos = cos_ref[...]
    sin = sin_ref[...]
    lane = lax.broadcasted_iota(jnp.int32, (1, LANES), 1)
    head0 = _is_head0(lane)

    def proj(i):
        return jnp.dot(hb, w_ref[:, IN_OFFSETS[i]:IN_OFFSETS[i + 1]], preferred_element_type=F32)

    def tiles(y):
        return [y[:, t * LANES:(t + 1) * LANES] for t in range(y.shape[1] // LANES)]

    def both_heads(t):
        return (jnp.where(head0, t, pltpu.roll(t, ROPE_HALF, 1)),
                jnp.where(head0, pltpu.roll(t, LANES - ROPE_HALF, 1), t))

    def heads(i, gain_row, out_ref, scale=None):
        for t, y in enumerate(tiles(proj(i))):
            o = _head_norm_rope(y, hn_ref[gain_row:gain_row + 1, :], cos, sin, lane)
            o = o if scale is None else o * scale
            out_ref[:, t * LANES:(t + 1) * LANES] = o.astype(BF16)

    def swa_k():
        k0, k1 = both_heads(_head_norm_rope(proj(1), hn_ref[1:2, :], cos, sin, lane))
        ka_ref[:, 0:LANES] = k0.astype(BF16)
        ka_ref[:, LANES:2 * LANES] = k1.astype(BF16)

    def values():
        tn = (((0,), (1,)), ((), ()))
        for i, out_ref in ((2, vat_ref), (5, vbt_ref)):
            out_ref[0] = lax.dot_general(w_ref[:, IN_OFFSETS[i]:IN_OFFSETS[i + 1]], hb, tn,
                                         preferred_element_type=F32).astype(BF16)

    def gate(i, out_ref, half):
        cols = slice(half * (D_MODEL // 2), (half + 1) * (D_MODEL // 2))
        y = jnp.dot(hb, w_ref[:, IN_OFFSETS[i] + cols.start:IN_OFFSETS[i] + cols.stop],
                    preferred_element_type=F32)
        out_ref[:, cols] = (0.5 * jnp.tanh(0.5 * y) + 0.5).astype(BF16)

    heads(0, 0, qa_ref, Q_SCALE)
    swa_k()
    heads(3, 2, qb_ref, Q_SCALE)
    heads(4, 3, kb_ref)
    values()
    gate(6, ga_ref, 0)
    gate(6, ga_ref, 1)
    gate(7, gb_ref, 0)
    gate(7, gb_ref, 1)


def _inproj_call(x2d, mod, g_mix, w_in, cos, sin, head_gains, layer, seq, tm):
    tokens, d = x2d.shape
    batch = tokens // seq
    steps_per_seq = seq // tm
    row = lambda i: (i, 0)
    of_layer = lambda i: (layer, 0, 0)
    seq_major = lambda i: (i // steps_per_seq, 0, i % steps_per_seq)

    def out(width, dtype=BF16):
        return jax.ShapeDtypeStruct((tokens, width), dtype)

    return pl.pallas_call(
        _inproj_kernel,
        grid=(tokens // tm,),
        in_specs=[
            pl.BlockSpec((tm, d), row),
            pl.BlockSpec((None, None, 6, d), lambda i: (layer, i // steps_per_seq, 0, 0)),
            pl.BlockSpec((None, 1, d), of_layer),
            pl.BlockSpec((None, d, IN_W), of_layer, pipeline_mode=pl.Buffered(1)),
            pl.BlockSpec((tm, LANES), row),
            pl.BlockSpec((tm, LANES), row),
            pl.BlockSpec((None, 4, LANES), of_layer),
        ],
        out_specs=[
            pl.BlockSpec((tm, SWA_Q_W), row),
            pl.BlockSpec((tm, 2 * LANES), row),
            pl.BlockSpec((1, SWA_KV_W, tm), seq_major),
            pl.BlockSpec((tm, MOBA_W), row),
            pl.BlockSpec((tm, MOBA_W), row),
            pl.BlockSpec((1, MOBA_W, tm), seq_major),
            pl.BlockSpec((tm, d), row),
            pl.BlockSpec((tm, d), row),
        ],
        out_shape=[
            out(SWA_Q_W), out(2 * LANES),
            jax.ShapeDtypeStruct((batch, SWA_KV_W, seq), BF16),
            out(MOBA_W), out(MOBA_W),
            jax.ShapeDtypeStruct((batch, MOBA_W, seq), BF16),
            out(d), out(d),
        ],
        compiler_params=pltpu.CompilerParams(vmem_limit_bytes=VMEM_LIMIT),
        name="in_proj",
    )(x2d, mod, g_mix, w_in, cos, sin, head_gains)


def _swa_kernel(sink_ref, q_ref, k_ref, vt_ref, o_ref, va_scr, s_scr, p_scr, tri_scr):
    W = SWA_WINDOW
    seq = q_ref.shape[1]
    group = SWA_Q_HEADS // SWA_KV_HEADS
    n_heads = HEADS_PER_TILE
    lane = lax.broadcasted_iota(jnp.int32, (1, LANES), 1)
    head0 = _is_head0(lane)
    nt = (((1,), (1,)), ((), ()))
    key = lax.broadcasted_iota(jnp.int32, (W, n_heads * W), 0)
    qry = lax.broadcasted_iota(jnp.int32, (W, n_heads * W), 1) % W
    from_prev = qry < key
    head_of_lane = lax.broadcasted_iota(jnp.int32, (1, n_heads * W), 1) // W
    tri_scr[0] = jnp.where(from_prev, 1.0, 0.0).astype(BF16)
    tri_scr[1] = jnp.where(from_prev, 0.0, 1.0).astype(BF16)

    ones = jnp.ones((SUM_ROWS, seq), BF16)
    for kvh in range(SWA_KV_HEADS):
        va_scr[kvh] = jnp.concatenate([vt_ref[0, kvh * HEAD_DIM:(kvh + 1) * HEAD_DIM, :], ones], axis=0)

    units = [(qi, t) for qi in range(seq // W) for t in range(SWA_Q_HEADS // n_heads)]
    col_max = {}

    def sink_row(t):
        s0, s1 = sink_ref[t * n_heads] * LOG2E, sink_ref[t * n_heads + 1] * LOG2E
        return jnp.where(head_of_lane == 0, s0, s1)

    def score_step(u):
        qi, t = units[u]
        kvh = (t * n_heads) // group
        q2 = q_ref[0, qi * W:(qi + 1) * W, t * LANES:(t + 1) * LANES]
        zero = jnp.zeros_like(q2)
        q_pair = jnp.concatenate([jnp.where(head0, q2, zero), jnp.where(head0, zero, q2)], axis=0)
        k0 = max(qi - 1, 0) * W
        s = lax.dot_general(k_ref[0, k0:k0 + 2 * W, kvh * LANES:(kvh + 1) * LANES], q_pair, nt,
                            preferred_element_type=F32)
        if qi == 0:
            s = jnp.where(from_prev, NEG_INF, s[0:W])
        else:
            s = jnp.where(from_prev, s[0:W], s[W:2 * W])
        s_scr[u % SWA_SLOTS] = s
        col_max[u] = jnp.maximum(jnp.max(s, axis=0, keepdims=True), sink_row(t))

    def exp_step(u):
        p = jnp.exp2(s_scr[u % SWA_SLOTS] - col_max[u]).astype(BF16)
        p_scr[u % SWA_SLOTS, 0:W, :] = p * tri_scr[0]
        p_scr[u % SWA_SLOTS, W:2 * W, :] = p * tri_scr[1]

    def value_step(u):
        qi, t = units[u]
        kvh = (t * n_heads) // group
        k0 = max(qi - 1, 0) * W
        if qi == 0:
            o = jnp.dot(va_scr[kvh, :, 0:W], p_scr[u % SWA_SLOTS, W:2 * W, :], preferred_element_type=F32)
        else:
            o = jnp.dot(va_scr[kvh, :, k0:k0 + 2 * W], p_scr[u % SWA_SLOTS], preferred_element_type=F32)
        denom = o[HEAD_DIM:HEAD_DIM + 1, :] + jnp.exp2(sink_row(t) - col_max.pop(u))
        res = o[0:HEAD_DIM] / denom
        tile = jnp.concatenate([res[:, j * W:(j + 1) * W] for j in range(n_heads)], axis=0)
        o_ref[0, qi * W:(qi + 1) * W, t * LANES:(t + 1) * LANES] = tile.T.astype(BF16)

    for r in range(len(units) + 2 * SWA_LAG):
        for fn, u in ((score_step, r), (exp_step, r - SWA_LAG), (value_step, r - 2 * SWA_LAG)):
            if 0 <= u < len(units):
                fn(u)


def _swa_call(sinks, qa, ka, vat):
    batch, seq, _ = qa.shape
    W = SWA_WINDOW
    assert seq % W == 0 and seq >= 2 * W and W == LANES
    return pl.pallas_call(
        _swa_kernel,
        grid=(batch,),
        in_specs=[
            pl.BlockSpec(memory_space=pltpu.SMEM),
            pl.BlockSpec((1, seq, SWA_Q_W), lambda b: (b, 0, 0)),
            pl.BlockSpec((1, seq, 2 * LANES), lambda b: (b, 0, 0)),
            pl.BlockSpec((1, SWA_KV_W, seq), lambda b: (b, 0, 0)),
        ],
        out_specs=pl.BlockSpec((1, seq, SWA_Q_W), lambda b: (b, 0, 0)),
        out_shape=jax.ShapeDtypeStruct((batch, seq, SWA_Q_W), BF16),
        scratch_shapes=[
            pltpu.VMEM((SWA_KV_HEADS, PV_ROWS, seq), BF16),
            pltpu.VMEM((SWA_SLOTS, W, HEADS_PER_TILE * W), F32),
            pltpu.VMEM((SWA_SLOTS, 2 * W, HEADS_PER_TILE * W), BF16),
            pltpu.VMEM((2, W, HEADS_PER_TILE * W), BF16),
        ],
        compiler_params=pltpu.CompilerParams(vmem_limit_bytes=VMEM_LIMIT),
        name="swa_attn",
    )(sinks, qa, ka, vat)


def _moba_offsets(col_max, col_sum, nblk):
    n_past = len(col_sum)
    keep = []
    for n in range(n_past):
        rank = jnp.where(col_sum[n] < NEG_INF, float(nblk - n_past), 0.0)
        for m in range(n_past):
            if m != n:
                before = (col_sum[m] >= col_sum[n]) if m < n else (col_sum[m] > col_sum[n])
                rank = rank + jnp.where(before, 1.0, 0.0)
        keep.append(rank < MOBA_TOPK)
    top = col_max[n_past]
    for n in range(n_past):
        top = jnp.maximum(top, jnp.where(keep[n], col_max[n], NEG_INF))
    return [jnp.where(keep[n], top, -NEG_INF) for n in range(n_past)] + [top]


def _moba_kernel(q_ref, k_ref, vt_ref, o_ref, qh_scr, va_scr, s_scr, p_scr):
    L = MOBA_BLOCK
    seq = q_ref.shape[1]
    nblk = seq // L
    lane = lax.broadcasted_iota(jnp.int32, (1, LANES), 1)
    head0 = _is_head0(lane)
    nt = (((1,), (1,)), ((), ()))
    causal_t = (lax.broadcasted_iota(jnp.int32, (L, L), 0) <= lax.broadcasted_iota(jnp.int32, (L, L), 1))

    q_all = q_ref[0]
    vt = vt_ref[0]
    ones = jnp.ones((SUM_ROWS, seq), BF16)
    for j in range(HEADS_PER_TILE):
        qh_scr[j] = jnp.where(head0 if j == 0 else ~head0, q_all, jnp.zeros_like(q_all))
        va_scr[j] = jnp.concatenate([vt[j * HEAD_DIM:(j + 1) * HEAD_DIM], ones], axis=0)

    units = [(qi, j) for qi in range(nblk) for j in range(HEADS_PER_TILE)]
    offsets, res = {}, {}

    def score_step(u):
        qi, j = units[u]
        s_all = lax.dot_general(k_ref[0, 0:(qi + 1) * L, :], qh_scr[j, qi * L:(qi + 1) * L, :], nt,
                                preferred_element_type=F32)
        col_max, col_sum = [], []
        for c in range(qi + 1):
            s = s_all[c * L:(c + 1) * L]
            if c == qi:
                s = jnp.where(causal_t, s, NEG_INF)
            else:
                col_sum.append(jnp.sum(s, axis=0, keepdims=True))
            s_scr[u % MOBA_SLOTS, c * L:(c + 1) * L, :] = s
            col_max.append(jnp.max(s, axis=0, keepdims=True))
        offsets[u] = _moba_offsets(col_max, col_sum, nblk)

    def exp_step(u, c):
        p_scr[u % MOBA_SLOTS, c * L:(c + 1) * L, :] = jnp.exp2(
            s_scr[u % MOBA_SLOTS, c * L:(c + 1) * L, :] - offsets[u][c]).astype(BF16)

    def value_step(u):
        qi, j = units[u]
        nk = (qi + 1) * L
        o = jnp.dot(va_scr[j, :, 0:nk], p_scr[u % MOBA_SLOTS, 0:nk, :],
                    preferred_element_type=F32)
        offsets.pop(u)
        res[j] = o[0:HEAD_DIM] / o[HEAD_DIM:HEAD_DIM + 1, :]
        if j == HEADS_PER_TILE - 1:
            o_ref[0, qi * L:(qi + 1) * L, :] = jnp.concatenate([res[0], res[1]], axis=0).T.astype(BF16)

    for r in range(len(units) + 2 * MOBA_LAG):
        if r < len(units):
            score_step(r)
        if 0 <= r - MOBA_LAG < len(units):
            for c in range(units[r - MOBA_LAG][0] + 1):
                exp_step(r - MOBA_LAG, c)
        if 0 <= r - 2 * MOBA_LAG < len(units):
            value_step(r - 2 * MOBA_LAG)


def _moba_call(qb, kb, vbt):
    batch, seq, width = qb.shape
    assert seq % MOBA_BLOCK == 0
    seq_spec = pl.BlockSpec((1, seq, LANES), lambda b, h: (b, 0, h))
    return pl.pallas_call(
        _moba_kernel,
        grid=(batch, width // LANES),
        in_specs=[seq_spec, seq_spec, pl.BlockSpec((1, LANES, seq), lambda b, h: (b, h, 0))],
        out_specs=seq_spec,
        out_shape=jax.ShapeDtypeStruct((batch, seq, width), BF16),
        scratch_shapes=[
            pltpu.VMEM((HEADS_PER_TILE, seq, LANES), BF16),
            pltpu.VMEM((HEADS_PER_TILE, PV_ROWS, seq), BF16),
            pltpu.VMEM((MOBA_SLOTS, seq, MOBA_BLOCK), F32),
            pltpu.VMEM((MOBA_SLOTS, seq, MOBA_BLOCK), BF16),
        ],
        compiler_params=pltpu.CompilerParams(vmem_limit_bytes=VMEM_LIMIT),
        name="moba_attn",
    )(qb, kb, vbt)


def _post_kernel(x_ref, oa_ref, ob_ref, ga_ref, gb_ref, mod_ref, g_ref,
                 woa_ref, wob_ref, wout_ref, wup_ref, wdn_ref, o_ref):
    ya = jnp.dot(oa_ref[...], woa_ref[...], preferred_element_type=F32)
    yb = jnp.dot(ob_ref[...], wob_ref[...], preferred_element_type=F32)
    mixed = ga_ref[...].astype(F32) * ya + gb_ref[...].astype(F32) * yb
    y = jnp.dot(mixed.astype(BF16), wout_ref[...], preferred_element_type=F32)
    x1 = x_ref[...] + mod_ref[2:3, :] * y
    h = _rms_adaln(x1, g_ref[...], mod_ref[3:4, :], mod_ref[4:5, :]).astype(BF16)
    acc = jnp.zeros(x1.shape, F32)
    for c in range(D_FF // FF_CHUNK):
        u = jnp.dot(h, wup_ref[:, c * FF_CHUNK:(c + 1) * FF_CHUNK], preferred_element_type=F32)
        u = jnp.square(jnp.maximum(u, 0.0)).astype(BF16)
        acc += jnp.dot(u, wdn_ref[c * FF_CHUNK:(c + 1) * FF_CHUNK, :], preferred_element_type=F32)
    o_ref[...] = x1 + mod_ref[5:6, :] * acc


def _post_call(x2d, oa, ob, ga, gb, mod, g_mlp, woa, wob, wout, wup, wdn, layer, seq, tm):
    tokens, d = x2d.shape
    steps_per_seq = seq // tm
    row = lambda i: (i, 0)
    of_layer = lambda i: (layer, 0, 0)

    def resident(w):
        return pl.BlockSpec((None,) + w.shape[1:], of_layer, pipeline_mode=pl.Buffered(1))

    return pl.pallas_call(
        _post_kernel,
        grid=(tokens // tm,),
        in_specs=[
            pl.BlockSpec((tm, d), row),
            pl.BlockSpec((tm, SWA_Q_W), row),
            pl.BlockSpec((tm, MOBA_W), row),
            pl.BlockSpec((tm, d), row),
            pl.BlockSpec((tm, d), row),
            pl.BlockSpec((None, None, 6, d), lambda i: (layer, i // steps_per_seq, 0, 0)),
            pl.BlockSpec((None, 1, d), of_layer),
            resident(woa), resident(wob), resident(wout), resident(wup), resident(wdn),
        ],
        out_specs=pl.BlockSpec((tm, d), row),
        out_shape=jax.ShapeDtypeStruct((tokens, d), F32),
        compiler_params=pltpu.CompilerParams(vmem_limit_bytes=VMEM_LIMIT),
        name="post_mlp",
    )(x2d, oa, ob, ga, gb, mod, g_mlp, woa, wob, wout, wup, wdn)


def kernel(x, c, positions, rms_mix, rms_mlp, w_ada, b_ada, w_in, q_norm_swa, k_norm_swa, q_norm_moba,
           k_norm_moba, swa_sinks, w_o_swa, w_o_moba, w_out, w_up, w_down):
    batch, seq, d = x.shape
    depth = w_in.shape[0]
    tokens = batch * seq
    assert seq % MOBA_BLOCK == 0 and d == D_MODEL

    mod = _ada_call(c, w_ada, b_ada).reshape(depth, batch, 6, d)
    cos, sin = _rope_tables(positions)
    sections = [w_in[..., IN_OFFSETS[i]:IN_OFFSETS[i + 1]] for i in range(len(IN_SPLITS))]
    w_in = jnp.concatenate([_pair_rope_halves(sec) if i in (0, 1, 3, 4) else sec
                            for i, sec in enumerate(sections)], axis=-1)
    w_in, w_o_swa, w_o_moba, w_out, w_up, w_down = (
        w.astype(BF16) for w in (w_in, w_o_swa, w_o_moba, w_out, w_up, w_down))
    gains = jnp.stack([_pair_rope_halves(jnp.tile(g, (1, HEADS_PER_TILE)))
                       for g in (q_norm_swa, k_norm_swa, q_norm_moba, k_norm_moba)], axis=1)
    rms_mix = rms_mix.reshape(depth, 1, d)
    rms_mlp = rms_mlp.reshape(depth, 1, d)

    x2d = x.reshape(tokens, d)
    for l in range(depth):
        qa, ka, vat, qb, kb, vbt, ga, gb = _inproj_call(
            x2d, mod, rms_mix, w_in, cos, sin, gains, l, seq, tm=2 * ROW_TILE)
        oa = _swa_call(swa_sinks[l], qa.reshape(batch, seq, -1), ka.reshape(batch, seq, -1), vat)
        ob = _moba_call(qb.reshape(batch, seq, -1), kb.reshape(batch, seq, -1), vbt)
        x2d = _post_call(x2d, oa.reshape(tokens, -1), ob.reshape(tokens, -1), ga, gb, mod, rms_mlp,
                         w_o_swa, w_o_moba, w_out, w_up, w_down, l, seq, tm=ROW_TILE)
    return x2d.reshape(batch, seq, d)
```
